```python
import jax, jax.numpy as jnp
from jax import lax
import numpy as np

D_MODEL = 1024
BATCH = 4
SEQ = 4096
DEPTH = 1

ATTN_HEADS = 8
ATTN_KV_HEADS = 2
ATTN_HEAD_DIM = 64
WINDOW = 128
DN_HEADS = 4
DN_KEY_DIM = 128
DN_VALUE_DIM = 128
CONV_WIDTH = 4
CHUNK = 64
D_FF = 4 * D_MODEL
LN_EPS = 1e-5
RMS_EPS = 1e-6
DEEPNORM_ALPHA = (2 * DEPTH) ** 0.25
DEEPNORM_BETA = (8 * DEPTH) ** -0.25

ATTN_Q_W = ATTN_HEADS * ATTN_HEAD_DIM
ATTN_KV_W = ATTN_KV_HEADS * ATTN_HEAD_DIM
DN_QK_W = DN_HEADS * DN_KEY_DIM
DN_V_W = DN_HEADS * DN_VALUE_DIM
DN_CONV_W = 2 * DN_QK_W + DN_V_W
GATE_W = 2 * D_MODEL
SPLIT_POINTS = (
    ATTN_Q_W,
    ATTN_Q_W + ATTN_KV_W,
    ATTN_Q_W + 2 * ATTN_KV_W,
    ATTN_Q_W + 2 * ATTN_KV_W + DN_CONV_W,
    ATTN_Q_W + 2 * ATTN_KV_W + DN_CONV_W + DN_V_W,
    ATTN_Q_W + 2 * ATTN_KV_W + DN_CONV_W + DN_V_W + DN_HEADS,
    ATTN_Q_W + 2 * ATTN_KV_W + DN_CONV_W + DN_V_W + 2 * DN_HEADS,
)
IN_WIDTH = ATTN_Q_W + 2 * ATTN_KV_W + DN_CONV_W + DN_V_W + 2 * DN_HEADS + GATE_W

kernel_name = "hybrid_swa_sink_gated_deltanet_deepnorm"


def layer_norm(x, g, b):
    xf = x.astype(jnp.float32)
    mu = jnp.mean(xf, axis=-1, keepdims=True)
    var = jnp.mean(jnp.square(xf - mu), axis=-1, keepdims=True)
    return ((xf - mu) * lax.rsqrt(var + LN_EPS) * g + b).astype(x.dtype)


def l2_normalize(x):
    xf = x.astype(jnp.float32)
    return xf * lax.rsqrt(jnp.sum(xf * xf, axis=-1, keepdims=True) + RMS_EPS)


def sliding_window_attention_with_sinks(q, k, v, sinks):
    B, T, Hq, hd = q.shape
    Hkv = k.shape[2]
    G = Hq // Hkv
    n = T // WINDOW
    qb = q.reshape(B, n, WINDOW, Hkv, G, hd)
    kb = k.reshape(B, n, WINDOW, Hkv, hd)
    vb = v.reshape(B, n, WINDOW, Hkv, hd)
    pad = ((0, 0), (1, 0), (0, 0), (0, 0), (0, 0))
    kk = jnp.concatenate([jnp.pad(kb, pad)[:, :-1], kb], axis=2)
    vv = jnp.concatenate([jnp.pad(vb, pad)[:, :-1], vb], axis=2)
    s = jnp.einsum('bnqhgd,bnkhd->bnhgqk', qb, kk).astype(jnp.float32) * (hd ** -0.5)
    qi = jnp.arange(WINDOW)[:, None] + WINDOW
    kj = jnp.arange(2 * WINDOW)[None, :]
    band = (kj <= qi) & (qi - kj < WINDOW)
    has_prev = (jnp.arange(n)[:, None, None] > 0) | (kj[None] >= WINDOW)
    mask = band[None] & has_prev
    s = jnp.where(mask[None, :, None, None], s, -1e30)
    sink = jnp.broadcast_to(sinks.astype(jnp.float32).reshape(1, 1, Hkv, G, 1, 1), s.shape[:-1] + (1,))
    p = jax.nn.softmax(jnp.concatenate([s, sink], axis=-1), axis=-1)[..., :-1]
    o = jnp.einsum('bnhgqk,bnkhd->bnqhgd', p.astype(v.dtype), vv)
    return o.reshape(B, T, Hq * hd)


def causal_depthwise_conv(x, w):
    K, C = w.shape
    return lax.conv_general_dilated(
        x, w[:, None, :].astype(x.dtype), window_strides=(1,), padding=[(K - 1, 0)],
        dimension_numbers=('NWC', 'WIO', 'NWC'), feature_group_count=C)


def gated_delta_rule_chunked(q, k, v, g, beta):
    B, T, H, dk = q.shape
    dv = v.shape[-1]
    n = T // CHUNK

    def chunks(t):
        return t.astype(jnp.float32).reshape(B, n, CHUNK, H, -1).transpose(1, 0, 3, 2, 4)

    q = chunks(q) * (dk ** -0.5)
    k = chunks(k)
    v = chunks(v)
    g = g.astype(jnp.float32).reshape(B, n, CHUNK, H).transpose(1, 0, 3, 2)
    beta = beta.astype(jnp.float32).reshape(B, n, CHUNK, H).transpose(1, 0, 3, 2)
    g = jnp.cumsum(g, axis=-1)
    causal = jnp.tril(jnp.ones((CHUNK, CHUNK), dtype=bool))
    strict = jnp.tril(jnp.ones((CHUNK, CHUNK), dtype=bool), -1)
    decay = jnp.exp(jnp.where(causal, g[..., :, None] - g[..., None, :], -jnp.inf))
    k_beta = k * beta[..., None]
    v_beta = v * beta[..., None]
    a = jnp.where(strict, jnp.einsum('nbhcd,nbhsd->nbhcs', k_beta, k) * decay, 0.0)
    eye = jnp.eye(CHUNK, dtype=jnp.float32)
    t_inv = lax.linalg.triangular_solve(eye + a, jnp.broadcast_to(eye, a.shape),
                                        left_side=True, lower=True, unit_diagonal=True)
    u = jnp.einsum('nbhcs,nbhse->nbhce', t_inv, v_beta)
    w = jnp.einsum('nbhcs,nbhsd->nbhcd', t_inv, k_beta * jnp.exp(g)[..., None])
    attn_intra = jnp.where(causal, jnp.einsum('nbhcd,nbhsd->nbhcs', q, k) * decay, 0.0)

    def step(state, inp):
        q_c, k_c, u_c, w_c, g_c, a_c = inp
        v_new = u_c - jnp.einsum('bhcd,bhde->bhce', w_c, state)
        o_c = (jnp.einsum('bhcd,bhde->bhce', q_c * jnp.exp(g_c)[..., None], state)
               + jnp.einsum('bhcs,bhse->bhce', a_c, v_new))
        g_last = g_c[..., -1]
        k_dec = k_c * jnp.exp(g_last[..., None] - g_c)[..., None]
        state = state * jnp.exp(g_last)[..., None, None] + jnp.einsum('bhcd,bhce->bhde', k_dec, v_new)
        return state, o_c

    state0 = jnp.zeros((B, H, dk, dv), dtype=jnp.float32)
    _, o = lax.scan(step, state0, (q, k, u, w, g, attn_intra))
    return o.transpose(1, 0, 3, 2, 4).reshape(B, T, H, dv)


def hybrid_mixer(x, w_in, conv_w, attn_sinks, dn_a_log, dn_dt_bias, dn_norm_w, w_attn_out, w_dn_out, w_out):
    B, T, _ = x.shape
    proj = x @ w_in
    aq, ak, av, dqkv, dz, db, da, gates = jnp.split(proj, SPLIT_POINTS, axis=-1)
    y_a = sliding_window_attention_with_sinks(
        aq.reshape(B, T, ATTN_HEADS, ATTN_HEAD_DIM),
        ak.reshape(B, T, ATTN_KV_HEADS, ATTN_HEAD_DIM),
        av.reshape(B, T, ATTN_KV_HEADS, ATTN_HEAD_DIM), attn_sinks) @ w_attn_out
    qkv = jax.nn.silu(causal_depthwise_conv(dqkv, conv_w))
    dq, dk_, dv_ = jnp.split(qkv, (DN_QK_W, 2 * DN_QK_W), axis=-1)
    dq = l2_normalize(dq.reshape(B, T, DN_HEADS, DN_KEY_DIM))
    dk_ = l2_normalize(dk_.reshape(B, T, DN_HEADS, DN_KEY_DIM))
    dv_ = dv_.reshape(B, T, DN_HEADS, DN_VALUE_DIM)
    g = -jnp.exp(dn_a_log.astype(jnp.float32)) * jax.nn.softplus(da.astype(jnp.float32) + dn_dt_bias.astype(jnp.float32))
    beta = jax.nn.sigmoid(db.astype(jnp.float32))
    o = gated_delta_rule_chunked(dq, dk_, dv_, g, beta)
    o = o * lax.rsqrt(jnp.mean(o * o, axis=-1, keepdims=True) + RMS_EPS) * dn_norm_w
    o = o * jax.nn.silu(dz.astype(jnp.float32).reshape(B, T, DN_HEADS, DN_VALUE_DIM))
    y_b = o.reshape(B, T, DN_V_W).astype(x.dtype) @ w_dn_out
    g_a, g_b = jnp.split(gates, 2, axis=-1)
    merged = jax.nn.sigmoid(g_a) * y_a + jax.nn.sigmoid(g_b) * y_b
    return merged @ w_out


def setup_inputs(seed: int = 0) -> dict:
    key = jax.random.key(seed)
    ks = jax.random.split(key, 16)
    f32 = jnp.float32
    x = jax.random.normal(ks[0], (BATCH, SEQ, D_MODEL), f32)
    w_in = jax.random.normal(ks[1], (DEPTH, D_MODEL, IN_WIDTH), f32) * D_MODEL ** -0.5
    conv_w = jax.random.normal(ks[2], (DEPTH, CONV_WIDTH, DN_CONV_W), f32) * CONV_WIDTH ** -0.5
    attn_sinks = jax.random.normal(ks[3], (DEPTH, ATTN_HEADS), f32) * 0.5
    dn_a_log = jnp.log(jax.random.uniform(ks[4], (DEPTH, DN_HEADS), f32, 1.0, 16.0))
    dt = jnp.exp(jax.random.uniform(ks[5], (DEPTH, DN_HEADS), f32, float(np.log(1e-3)), float(np.log(1e-1))))
    dn_dt_bias = dt + jnp.log(-jnp.expm1(-dt))
    dn_norm_w = 1.0 + 0.02 * jax.random.normal(ks[6], (DEPTH, DN_VALUE_DIM), f32)
    w_attn_out = jax.random.normal(ks[7], (DEPTH, ATTN_Q_W, D_MODEL), f32) * ATTN_Q_W ** -0.5
    w_dn_out = jax.random.normal(ks[8], (DEPTH, DN_V_W, D_MODEL), f32) * DN_V_W ** -0.5
    w_out = jax.random.normal(ks[9], (DEPTH, D_MODEL, D_MODEL), f32) * (D_MODEL ** -0.5 * DEEPNORM_BETA)
    ln1_g = 1.0 + 0.02 * jax.random.normal(ks[10], (DEPTH, D_MODEL), f32)
    ln1_b = 0.02 * jax.random.normal(ks[11], (DEPTH, D_MODEL), f32)
    w_up = jax.random.normal(ks[12], (DEPTH, D_MODEL, D_FF), f32) * D_MODEL ** -0.5
    w_down = jax.random.normal(ks[13], (DEPTH, D_FF, D_MODEL), f32) * (D_FF ** -0.5 * DEEPNORM_BETA)
    ln2_g = 1.0 + 0.02 * jax.random.normal(ks[14], (DEPTH, D_MODEL), f32)
    ln2_b = 0.02 * jax.random.normal(ks[15], (DEPTH, D_MODEL), f32)
    return {"x": x, "w_in": w_in, "conv_w": conv_w, "attn_sinks": attn_sinks,
            "dn_a_log": dn_a_log, "dn_dt_bias": dn_dt_bias, "dn_norm_w": dn_norm_w,
            "w_attn_out": w_attn_out, "w_dn_out": w_dn_out, "w_out": w_out,
            "ln1_g": ln1_g, "ln1_b": ln1_b, "w_up": w_up, "w_down": w_down,
            "ln2_g": ln2_g, "ln2_b": ln2_b}


def reference(x, w_in, conv_w, attn_sinks, dn_a_log, dn_dt_bias, dn_norm_w, w_attn_out, w_dn_out, w_out,
              ln1_g, ln1_b, w_up, w_down, ln2_g, ln2_b):
    for l in range(DEPTH):
        mix = hybrid_mixer(x, w_in[l], conv_w[l], attn_sinks[l], dn_a_log[l], dn_dt_bias[l], dn_norm_w[l],
                           w_attn_out[l], w_dn_out[l], w_out[l])
        x = layer_norm(DEEPNORM_ALPHA * x + mix, ln1_g[l], ln1_b[l])
        h = jnp.square(jax.nn.relu(x @ w_up[l])) @ w_down[l]
        x = layer_norm(DEEPNORM_ALPHA * x + h, ln2_g[l], ln2_b[l])
    return x
```

```python
import functools

import jax
import jax.numpy as jnp
from jax import lax
from jax.experimental import pallas as pl
from jax.experimental.pallas import tpu as pltpu

D_MODEL = 1024
ATTN_HEADS = 8
ATTN_KV_HEADS = 2
ATTN_HEAD_DIM = 64
WINDOW = 128
DN_HEADS = 4
DN_KEY_DIM = 128
DN_VALUE_DIM = 128
CONV_WIDTH = 4
CHUNK = 64
D_FF = 4 * D_MODEL
LN_EPS = 1e-5
RMS_EPS = 1e-6
DEPTH = 1
DEEPNORM_ALPHA = (2 * DEPTH) ** 0.25

ATTN_Q_W = ATTN_HEADS * ATTN_HEAD_DIM
ATTN_KV_W = ATTN_KV_HEADS * ATTN_HEAD_DIM
DN_QK_W = DN_HEADS * DN_KEY_DIM
DN_V_W = DN_HEADS * DN_VALUE_DIM
DN_CONV_W = 2 * DN_QK_W + DN_V_W

LANES = 128
SUBLANES = 8
VMEM_LIMIT = 56 * 1024 * 1024

PROJ_TM = 512
ATTN_TQ = 512
DN_STEP = 128
FFN_TM = 256

BF16 = jnp.bfloat16
F32 = jnp.float32


def _dot(a, b):
    return jnp.dot(a, b, preferred_element_type=F32)


def _dot_nt(a, b):
    return lax.dot_general(a, b, (((1,), (1,)), ((), ())), preferred_element_type=F32)


def _dot_tn(a, b):
    return lax.dot_general(a, b, (((0,), (0,)), ((), ())), preferred_element_type=F32)


def _silu(x):
    return x / (1.0 + jnp.exp(-x))


def _sigmoid(x):
    return 1.0 / (1.0 + jnp.exp(-x))


def _const_spec(shape):
    return pl.BlockSpec(shape, lambda *_: (0,) * len(shape), pipeline_mode=pl.Buffered(1))


def _proj_kernel(x_ref, w_attn_ref, w_dqkv_ref, w_dz_ref, w_ba_ref, conv_ref, gpar_ref,
                 aqkv_ref, dq_ref, dk_ref, dv_ref, dzs_ref, gbc_ref, gbr_ref,
                 halo_ref, *, tiles_per_seq):
    i = pl.program_id(0)
    tm = x_ref.shape[0]
    xb = x_ref[...].astype(BF16)

    a = _dot(xb, w_attn_ref[...])
    aqkv_ref[:, :ATTN_Q_W] = (a[:, :ATTN_Q_W] * (ATTN_HEAD_DIM ** -0.5)).astype(BF16)
    aqkv_ref[:, ATTN_Q_W:] = a[:, ATTN_Q_W:].astype(BF16)

    dzs_ref[...] = _silu(_dot(xb, w_dz_ref[...])).astype(BF16)

    p = _dot(xb, w_dqkv_ref[...])

    @pl.when(i % tiles_per_seq == 0)
    def _():
        halo_ref[...] = jnp.zeros_like(halo_ref)

    halo = halo_ref[...]
    cw = conv_ref[...]
    row8 = lax.broadcasted_iota(jnp.int32, (SUBLANES, DN_CONV_W), 0)
    y = p * cw[CONV_WIDTH - 1:CONV_WIDTH, :]
    y_top = p[:SUBLANES] * cw[CONV_WIDTH - 1:CONV_WIDTH, :]
    for s in range(1, CONV_WIDTH):
        w_s = cw[CONV_WIDTH - 1 - s:CONV_WIDTH - s, :]
        rolled = pltpu.roll(p, s, 0)
        y = y + rolled * w_s
        fix = jnp.where(row8 < s, pltpu.roll(halo, s, 0), rolled[:SUBLANES])
        y_top = y_top + fix * w_s
    halo_ref[...] = p[tm - SUBLANES:, :]

    def finish(yy, rows):
        yy = _silu(yy)
        for h in range(DN_HEADS):
            lo = h * DN_KEY_DIM
            qh = yy[:, lo:lo + DN_KEY_DIM]
            qh = qh * lax.rsqrt(jnp.sum(qh * qh, axis=-1, keepdims=True) + RMS_EPS)
            dq_ref[rows, lo:lo + DN_KEY_DIM] = (qh * (DN_KEY_DIM ** -0.5)).astype(BF16)
            kh = yy[:, DN_QK_W + lo:DN_QK_W + lo + DN_KEY_DIM]
            kh = kh * lax.rsqrt(jnp.sum(kh * kh, axis=-1, keepdims=True) + RMS_EPS)
            dk_ref[rows, lo:lo + DN_KEY_DIM] = kh.astype(BF16)
        dv_ref[rows, :] = yy[:, 2 * DN_QK_W:].astype(BF16)

    finish(y, slice(None))
    finish(y_top, slice(0, SUBLANES))

    ba = _dot(xb, w_ba_ref[...])
    gpar = gpar_ref[...]
    lane = lax.broadcasted_iota(jnp.int32, (tm, LANES), 1)
    row = lax.broadcasted_iota(jnp.int32, (tm, LANES), 0)
    beta = _sigmoid(ba)
    z = ba + gpar[1:2, :]
    softplus = jnp.maximum(z, 0.0) + jnp.log1p(jnp.exp(-jnp.abs(z)))
    g = -jnp.exp(gpar[0:1, :]) * softplus
    s = 1
    while s < CHUNK:
        g = g + jnp.where(row % CHUNK >= s, pltpu.roll(g, s, 0), 0.0)
        s *= 2
    gb = jnp.where(lane < DN_HEADS, beta, jnp.where(lane < 2 * DN_HEADS, g, 0.0))
    gbc_ref[...] = gb
    gbr_ref[...] = gb.T[:SUBLANES, :]


def _project(x2, w_attn, w_dqkv, w_dz, w_ba, conv_w, gpar, batch, seq):
    n = x2.shape[0]
    tm = PROJ_TM
    tiles_per_seq = seq // tm
    attn_w = w_attn.shape[1]
    row_spec = lambda width: pl.BlockSpec((tm, width), lambda i: (i, 0))
    out_shape = (
        jax.ShapeDtypeStruct((n, attn_w), BF16),
        jax.ShapeDtypeStruct((n, DN_QK_W), BF16),
        jax.ShapeDtypeStruct((n, DN_QK_W), BF16),
        jax.ShapeDtypeStruct((n, DN_V_W), BF16),
        jax.ShapeDtypeStruct((n, DN_V_W), BF16),
        jax.ShapeDtypeStruct((n, LANES), F32),
        jax.ShapeDtypeStruct((batch, SUBLANES, seq), F32),
    )
    return pl.pallas_call(
        functools.partial(_proj_kernel, tiles_per_seq=tiles_per_seq),
        grid=(n // tm,),
        in_specs=[
            row_spec(D_MODEL),
            _const_spec(w_attn.shape), _const_spec(w_dqkv.shape), _const_spec(w_dz.shape),
            _const_spec(w_ba.shape), _const_spec(conv_w.shape), _const_spec(gpar.shape),
        ],
        out_specs=(
            row_spec(attn_w), row_spec(DN_QK_W), row_spec(DN_QK_W), row_spec(DN_V_W), row_spec(DN_V_W),
            row_spec(LANES),
            pl.BlockSpec((None, SUBLANES, tm), lambda i: (i // tiles_per_seq, 0, i % tiles_per_seq)),
        ),
        out_shape=out_shape,
        scratch_shapes=[pltpu.VMEM((SUBLANES, DN_CONV_W), F32)],
        compiler_params=pltpu.CompilerParams(dimension_semantics=("arbitrary",), vmem_limit_bytes=VMEM_LIMIT),
        name="proj_dnprep",
    )(x2, w_attn, w_dqkv, w_dz, w_ba, conv_w, gpar)


def _attn_kernel(sink_ref, q_ref, kv_ref, kvp_ref, o_ref):
    j = pl.program_id(1)
    tq = q_ref.shape[0]
    w = WINDOW
    pair_w = 2 * ATTN_HEAD_DIM
    lane = lax.broadcasted_iota(jnp.int32, (2 * w, pair_w), 1)
    lo = lane < ATTN_HEAD_DIM
    qi = lax.broadcasted_iota(jnp.int32, (w, 2 * w), 0)
    kj = lax.broadcasted_iota(jnp.int32, (w, 2 * w), 1)
    band = (kj > qi) & (kj <= qi + w)
    lane_o = lax.broadcasted_iota(jnp.int32, (w, pair_w), 1)
    lo_o = lane_o < ATTN_HEAD_DIM
    zero = jnp.zeros((), BF16)

    for wi in range(tq // w):
        if wi == 0:
            kv_prev = kvp_ref[...]
            mask = band & ((kj >= w) | (j > 0))
        else:
            kv_prev = kv_ref[(wi - 1) * w:wi * w, :]
            mask = band
        kv2 = jnp.concatenate([kv_prev, kv_ref[wi * w:(wi + 1) * w, :]], axis=0)
        q = q_ref[wi * w:(wi + 1) * w, :]
        for hk in range(ATTN_KV_HEADS):
            kd = kv2[:, hk * pair_w:(hk + 1) * pair_w]
            vd = kv2[:, (ATTN_KV_HEADS + hk) * pair_w:(ATTN_KV_HEADS + hk + 1) * pair_w]
            kbd = jnp.concatenate([jnp.where(lo, kd, zero), jnp.where(lo, zero, kd)], axis=0)
            vbd = jnp.concatenate([jnp.where(lo, vd, zero), jnp.where(lo, zero, vd)], axis=0)
            for pp in range(ATTN_HEADS // ATTN_KV_HEADS // 2):
                pair = hk * (ATTN_HEADS // ATTN_KV_HEADS // 2) + pp
                qp = q[:, pair * pair_w:(pair + 1) * pair_w]
                s = _dot_nt(qp, kbd)
                ps, invs = [], []
                for e in range(2):
                    se = jnp.where(mask, s[:, e * 2 * w:(e + 1) * 2 * w], -1e30)
                    sink = sink_ref[2 * pair + e]
                    m = jnp.maximum(jnp.max(se, axis=-1, keepdims=True), sink)
                    pe = jnp.exp(se - m)
                    den = jnp.sum(pe, axis=-1, keepdims=True) + jnp.exp(sink - m)
                    ps.append(pe.astype(BF16))
                    invs.append(1.0 / den)
                o = _dot(jnp.concatenate(ps, axis=1), vbd)
                o = o * jnp.where(lo_o, invs[0], invs[1])
                o_ref[wi * w:(wi + 1) * w, pair * pair_w:(pair + 1) * pair_w] = o.astype(BF16)


def _attention(aqkv, sinks, batch, seq):
    n = aqkv.shape[0]
    tq = ATTN_TQ
    nq = seq // tq
    wpt = tq // WINDOW
    kv_w = aqkv.shape[1] - ATTN_Q_W
    assert kv_w == ATTN_Q_W
    return pl.pallas_call(
        _attn_kernel,
        grid=(batch, nq),
        in_specs=[
            pl.BlockSpec(memory_space=pltpu.SMEM),
            pl.BlockSpec((tq, ATTN_Q_W), lambda b, j: (b * nq + j, 0)),
            pl.BlockSpec((tq, kv_w), lambda b, j: (b * nq + j, 1)),
            pl.BlockSpec((WINDOW, kv_w), lambda b, j: (jnp.maximum((b * nq + j) * wpt - 1, 0), 1)),
        ],
        out_specs=pl.BlockSpec((tq, ATTN_Q_W), lambda b, j: (b * nq + j, 0)),
        out_shape=jax.ShapeDtypeStruct((n, ATTN_Q_W), BF16),
        compiler_params=pltpu.CompilerParams(dimension_semantics=("arbitrary", "arbitrary"),
                                             vmem_limit_bytes=VMEM_LIMIT),
        name="swa_sinks",
    )(sinks, aqkv, aqkv, aqkv)


def _split_dot(a, b):
    ah = a.astype(BF16)
    al = (a - ah.astype(F32)).astype(BF16)
    bh = b.astype(BF16)
    bl = (b - bh.astype(F32)).astype(BF16)
    return _dot(ah, bh) + (_dot(ah, bl) + _dot(al, bh))


def _unit_lower_inverse(a, row, col):
    eye = (row == col).astype(F32)
    blk16 = (row // 16) == (col // 16)
    blk32 = (row // 32) == (col // 32)
    n1 = jnp.where(blk16, -a, 0.0)
    n2 = _split_dot(n1, n1)
    n4 = _split_dot(n2, n2)
    n8 = _split_dot(n4, n4)
    t = eye + n1
    t = t + _split_dot(t, n2)
    t = t + _split_dot(t, n4)
    t = t + _split_dot(t, n8)
    off1 = jnp.where(blk32 & jnp.logical_not(blk16), a, 0.0)
    t = t - _split_dot(_split_dot(t, off1), t)
    off2 = jnp.where(blk32, 0.0, a)
    t = t - _split_dot(_split_dot(t, off2), t)
    return t


def _delta_kernel(q_ref, k_ref, v_ref, z_ref, gbc_ref, gbr_ref, nw_ref, o_ref, state_ref):
    step = pl.program_id(0)
    batch = q_ref.shape[0]
    c = CHUNK

    @pl.when(step == 0)
    def _():
        state_ref[...] = jnp.zeros_like(state_ref)

    row = lax.broadcasted_iota(jnp.int32, (c, c), 0)
    col = lax.broadcasted_iota(jnp.int32, (c, c), 1)
    causal = row >= col
    strict = row > col
    nw = nw_ref[...]

    for cc in range(q_ref.shape[1] // c):
        r0 = cc * c
        for b in range(batch):
            gbc = gbc_ref[b, r0:r0 + c, :]
            gbr = gbr_ref[b, :, r0:r0 + c]
            for h in range(DN_HEADS):
                lo = h * DN_KEY_DIM
                q = q_ref[b, r0:r0 + c, lo:lo + DN_KEY_DIM]
                k = k_ref[b, r0:r0 + c, lo:lo + DN_KEY_DIM]
                v = v_ref[b, r0:r0 + c, lo:lo + DN_VALUE_DIM]
                beta_c = gbc[:, h:h + 1]
                g_c = gbc[:, DN_HEADS + h:DN_HEADS + h + 1]
                beta_r = gbr[h:h + 1, :]
                g_r = gbr[DN_HEADS + h:DN_HEADS + h + 1, :]
                g_last = g_c[c - 1:c, :]

                decay = jnp.exp(jnp.where(causal, g_c - g_r, -1e30))
                kk = _dot_nt(k, k)
                qk = _dot_nt(q, k)
                a = jnp.where(strict, beta_c * kk * decay, 0.0)
                attn = qk * decay
                t = _unit_lower_inverse(a, row, col)
                u = _dot((t * beta_r).astype(BF16), v)
                w = _dot((t * (beta_r * jnp.exp(g_r))).astype(BF16), k)

                idx = b * DN_HEADS + h
                state = state_ref[idx]
                sb = state.astype(BF16)
                v_new = u - _dot(w.astype(BF16), sb)
                vb = v_new.astype(BF16)
                kf = k.astype(F32)
                qg = (q.astype(F32) * jnp.exp(g_c)).astype(BF16)
                o = _dot(qg, sb) + _dot(attn.astype(BF16), vb)
                k_dec = (kf * jnp.exp(g_last - g_c)).astype(BF16)
                state_ref[idx] = state * jnp.exp(g_last) + _dot_tn(k_dec, vb)

                o = o * lax.rsqrt(jnp.mean(o * o, axis=-1, keepdims=True) + RMS_EPS) * nw
                o = o * z_ref[b, r0:r0 + c, lo:lo + DN_VALUE_DIM].astype(F32)
                o_ref[b, r0:r0 + c, lo:lo + DN_VALUE_DIM] = o.astype(BF16)


def _delta_rule(dq, dk, dv, dzs, gbc, gbr, norm_w, batch, seq):
    ts = DN_STEP
    tok_spec = lambda width: pl.BlockSpec((batch, ts, width), lambda i: (0, i, 0))
    return pl.pallas_call(
        _delta_kernel,
        grid=(seq // ts,),
        in_specs=[
            tok_spec(DN_QK_W), tok_spec(DN_QK_W), tok_spec(DN_V_W), tok_spec(DN_V_W), tok_spec(LANES),
            pl.BlockSpec((batch, SUBLANES, ts), lambda i: (0, 0, i)),
            _const_spec(norm_w.shape),
        ],
        out_specs=tok_spec(DN_V_W),
        out_shape=jax.ShapeDtypeStruct((batch, seq, DN_V_W), BF16),
        scratch_shapes=[pltpu.VMEM((batch * DN_HEADS, DN_KEY_DIM, DN_VALUE_DIM), F32)],
        compiler_params=pltpu.CompilerParams(dimension_semantics=("arbitrary",), vmem_limit_bytes=VMEM_LIMIT),
        name="gated_delta_rule",
    )(dq, dk, dv, dzs, gbc, gbr, norm_w)


def _layer_norm(x, g, b):
    mu = jnp.mean(x, axis=-1, keepdims=True)
    xc = x - mu
    var = jnp.mean(xc * xc, axis=-1, keepdims=True)
    return xc * lax.rsqrt(var + LN_EPS) * g + b


def _merge_ffn_kernel(x_ref, oa_ref, ob_ref, wg_ref, wao_ref, wdo_ref, wout_ref, ln1_ref,
                      wup_ref, wdown_ref, ln2_ref, out_ref):
    x = x_ref[...]
    gates = _dot(x.astype(BF16), wg_ref[...])
    ya = _dot(oa_ref[...], wao_ref[...])
    yb = _dot(ob_ref[...], wdo_ref[...])
    merged = _sigmoid(gates[:, :D_MODEL]) * ya + _sigmoid(gates[:, D_MODEL:]) * yb
    mix = _dot(merged.astype(BF16), wout_ref[...])
    ln1 = ln1_ref[...]
    x1 = _layer_norm(DEEPNORM_ALPHA * x + mix, ln1[0:1, :], ln1[1:2, :])
    h = jnp.maximum(_dot(x1.astype(BF16), wup_ref[...]), 0.0)
    h = _dot((h * h).astype(BF16), wdown_ref[...])
    ln2 = ln2_ref[...]
    out_ref[...] = _layer_norm(DEEPNORM_ALPHA * x1 + h, ln2[0:1, :], ln2[1:2, :])


def _merge_ffn(x2, oa, ob, wg, wao, wdo, wout, ln1, wup, wdown, ln2):
    n = x2.shape[0]
    tm = FFN_TM
    row_spec = lambda width: pl.BlockSpec((tm, width), lambda i: (i, 0))
    return pl.pallas_call(
        _merge_ffn_kernel,
        grid=(n // tm,),
        in_specs=[
            row_spec(D_MODEL), row_spec(ATTN_Q_W), row_spec(DN_V_W),
            _const_spec(wg.shape), _const_spec(wao.shape), _const_spec(wdo.shape), _const_spec(wout.shape),
            _const_spec(ln1.shape), _const_spec(wup.shape), _const_spec(wdown.shape), _const_spec(ln2.shape),
        ],
        out_specs=row_spec(D_MODEL),
        out_shape=jax.ShapeDtypeStruct((n, D_MODEL), F32),
        compiler_params=pltpu.CompilerParams(dimension_semantics=("arbitrary",), vmem_limit_bytes=VMEM_LIMIT),
        name="merge_ffn",
    )(x2, oa, ob, wg, wao, wdo, wout, ln1, wup, wdown, ln2)


def _layer(x, w_in, conv_w, attn_sinks, dn_a_log, dn_dt_bias, dn_norm_w, w_attn_out, w_dn_out, w_out,
           ln1_g, ln1_b, w_up, w_down, ln2_g, ln2_b):
    batch, seq, _ = x.shape
    n = batch * seq
    x2 = x.reshape(n, D_MODEL)

    c0 = ATTN_Q_W
    c1 = c0 + ATTN_KV_W
    c2 = c1 + ATTN_KV_W
    c3 = c2 + DN_CONV_W
    c4 = c3 + DN_V_W
    c5 = c4 + 2 * DN_HEADS
    wb = w_in.astype(BF16)
    hd = ATTN_HEAD_DIM
    dup = lambda wcols: jnp.concatenate(
        [wcols[:, (hh // 2) * hd:(hh // 2 + 1) * hd] for hh in range(2 * ATTN_KV_HEADS)], axis=1)
    w_attn = jnp.concatenate([wb[:, :c0], dup(wb[:, c0:c1]), dup(wb[:, c1:c2])], axis=1)
    w_dqkv = wb[:, c2:c3]
    w_dz = wb[:, c3:c4]
    w_ba = jnp.pad(wb[:, c4:c5], ((0, 0), (0, LANES - 2 * DN_HEADS)))
    w_gates = wb[:, c5:]
    gpar = jnp.zeros((2, LANES), F32)
    gpar = gpar.at[0, DN_HEADS:2 * DN_HEADS].set(dn_a_log.astype(F32))
    gpar = gpar.at[1, DN_HEADS:2 * DN_HEADS].set(dn_dt_bias.astype(F32))

    aqkv, dq, dk, dv, dzs, gbc, gbr = _project(x2, w_attn, w_dqkv, w_dz, w_ba, conv_w.astype(F32), gpar, batch, seq)
    oa = _attention(aqkv, attn_sinks.astype(F32), batch, seq)
    r3 = lambda t: t.reshape(batch, seq, t.shape[-1])
    ob = _delta_rule(r3(dq), r3(dk), r3(dv), r3(dzs), r3(gbc), gbr,
                     dn_norm_w.astype(F32).reshape(1, DN_VALUE_DIM), batch, seq)
    out = _merge_ffn(
        x2, oa, ob.reshape(n, DN_V_W), w_gates, w_attn_out.astype(BF16), w_dn_out.astype(BF16),
        w_out.astype(BF16), jnp.stack([ln1_g, ln1_b]).astype(F32), w_up.astype(BF16), w_down.astype(BF16),
        jnp.stack([ln2_g, ln2_b]).astype(F32))
    return out.reshape(batch, seq, D_MODEL)


def kernel(x, w_in, conv_w, attn_sinks, dn_a_log, dn_dt_bias, dn_norm_w, w_attn_out, w_dn_out, w_out,
           ln1_g, ln1_b, w_up, w_down, ln2_g, ln2_b):
    for l in range(DEPTH):
        x = _layer(x, w_in[l], conv_w[l], attn_sinks[l], dn_a_log[l], dn_dt_bias[l], dn_norm_w[l],
                   w_attn_out[l], w_dn_out[l], w_out[l], ln1_g[l], ln1_b[l], w_up[l], w_down[l],
                   ln2_g[l], ln2_b[l])
    return x
```

```python
import functools

import jax
import jax.numpy as jnp
from jax import lax
from jax.experimental import pallas as pl
from jax.experimental.pallas import tpu as pltpu

D_MODEL = 1024
ATTN_HEADS = 8
ATTN_KV_HEADS = 2
ATTN_HEAD_DIM = 64
WINDOW = 128
DN_HEADS = 4
DN_KEY_DIM = 128
DN_VALUE_DIM = 128
CONV_WIDTH = 4
CHUNK = 64
D_FF = 4 * D_MODEL
LN_EPS = 1e-5
RMS_EPS = 1e-6
DEPTH = 1
DEEPNORM_ALPHA = (2 * DEPTH) ** 0.25

ATTN_Q_W = ATTN_HEADS * ATTN_HEAD_DIM
ATTN_KV_W = ATTN_KV_HEADS * ATTN_HEAD_DIM
DN_QK_W = DN_HEADS * DN_KEY_DIM
DN_V_W = DN_HEADS * DN_VALUE_DIM
DN_CONV_W = 2 * DN_QK_W + DN_V_W

LANES = 128
SUBLANES = 8
VMEM_LIMIT = 56 * 1024 * 1024

PROJ_TM = 512
ATTN_TQ = 512
DN_PREP_STEP = 512
DN_STEP = 128
FFN_TM = 256

BF16 = jnp.bfloat16
F32 = jnp.float32


def _dot(a, b):
    return jnp.dot(a, b, preferred_element_type=F32)


def _dot_nt(a, b):
    return lax.dot_general(a, b, (((1,), (1,)), ((), ())), preferred_element_type=F32)


def _dot_tn(a, b):
    return lax.dot_general(a, b, (((0,), (0,)), ((), ())), preferred_element_type=F32)


def _silu(x):
    return x / (1.0 + jnp.exp(-x))


def _sigmoid(x):
    return 1.0 / (1.0 + jnp.exp(-x))


def _const_spec(shape):
    return pl.BlockSpec(shape, lambda *_: (0,) * len(shape), pipeline_mode=pl.Buffered(1))


def _proj_kernel(x_ref, w_attn_ref, w_dqkv_ref, w_dz_ref, w_ba_ref, conv_ref, gpar_ref,
                 aqkv_ref, dq_ref, dk_ref, dv_ref, dzs_ref, gbc_ref, gbr_ref,
                 halo_ref, *, tiles_per_seq):
    i = pl.program_id(0)
    tm = x_ref.shape[0]
    xb = x_ref[...].astype(BF16)

    a = _dot(xb, w_attn_ref[...])
    aqkv_ref[:, :ATTN_Q_W] = (a[:, :ATTN_Q_W] * (ATTN_HEAD_DIM ** -0.5)).astype(BF16)
    aqkv_ref[:, ATTN_Q_W:] = a[:, ATTN_Q_W:].astype(BF16)

    dzs_ref[...] = _silu(_dot(xb, w_dz_ref[...])).astype(BF16)

    p = _dot(xb, w_dqkv_ref[...])

    @pl.when(i % tiles_per_seq == 0)
    def _():
        halo_ref[...] = jnp.zeros_like(halo_ref)

    halo = halo_ref[...]
    cw = conv_ref[...]
    row8 = lax.broadcasted_iota(jnp.int32, (SUBLANES, DN_CONV_W), 0)
    y = p * cw[CONV_WIDTH - 1:CONV_WIDTH, :]
    y_top = p[:SUBLANES] * cw[CONV_WIDTH - 1:CONV_WIDTH, :]
    for s in range(1, CONV_WIDTH):
        w_s = cw[CONV_WIDTH - 1 - s:CONV_WIDTH - s, :]
        rolled = pltpu.roll(p, s, 0)
        y = y + rolled * w_s
        fix = jnp.where(row8 < s, pltpu.roll(halo, s, 0), rolled[:SUBLANES])
        y_top = y_top + fix * w_s
    halo_ref[...] = p[tm - SUBLANES:, :]

    def finish(yy, rows):
        yy = _silu(yy)
        for h in range(DN_HEADS):
            lo = h * DN_KEY_DIM
            qh = yy[:, lo:lo + DN_KEY_DIM]
            qh = qh * lax.rsqrt(jnp.sum(qh * qh, axis=-1, keepdims=True) + RMS_EPS)
            dq_ref[rows, lo:lo + DN_KEY_DIM] = (qh * (DN_KEY_DIM ** -0.5)).astype(BF16)
            kh = yy[:, DN_QK_W + lo:DN_QK_W + lo + DN_KEY_DIM]
            kh = kh * lax.rsqrt(jnp.sum(kh * kh, axis=-1, keepdims=True) + RMS_EPS)
            dk_ref[rows, lo:lo + DN_KEY_DIM] = kh.astype(BF16)
        dv_ref[rows, :] = yy[:, 2 * DN_QK_W:].astype(BF16)

    finish(y, slice(None))
    finish(y_top, slice(0, SUBLANES))

    ba = _dot(xb, w_ba_ref[...])
    gpar = gpar_ref[...]
    lane = lax.broadcasted_iota(jnp.int32, (tm, LANES), 1)
    row = lax.broadcasted_iota(jnp.int32, (tm, LANES), 0)
    beta = _sigmoid(ba)
    z = ba + gpar[1:2, :]
    softplus = jnp.maximum(z, 0.0) + jnp.log1p(jnp.exp(-jnp.abs(z)))
    g = -jnp.exp(gpar[0:1, :]) * softplus
    s = 1
    while s < CHUNK:
        g = g + jnp.where(row % CHUNK >= s, pltpu.roll(g, s, 0), 0.0)
        s *= 2
    gb = jnp.where(lane < DN_HEADS, beta, jnp.where(lane < 2 * DN_HEADS, g, 0.0))
    gbc_ref[...] = gb
    gbr_ref[...] = gb.T[:SUBLANES, :]


def _project(x2, w_attn, w_dqkv, w_dz, w_ba, conv_w, gpar, batch, seq):
    n = x2.shape[0]
    tm = PROJ_TM
    tiles_per_seq = seq // tm
    attn_w = w_attn.shape[1]
    row_spec = lambda width: pl.BlockSpec((tm, width), lambda i: (i, 0))
    out_shape = (
        jax.ShapeDtypeStruct((n, attn_w), BF16),
        jax.ShapeDtypeStruct((n, DN_QK_W), BF16),
        jax.ShapeDtypeStruct((n, DN_QK_W), BF16),
        jax.ShapeDtypeStruct((n, DN_V_W), BF16),
        jax.ShapeDtypeStruct((n, DN_V_W), BF16),
        jax.ShapeDtypeStruct((n, LANES), F32),
        jax.ShapeDtypeStruct((batch, SUBLANES, seq), F32),
    )
    return pl.pallas_call(
        functools.partial(_proj_kernel, tiles_per_seq=tiles_per_seq),
        grid=(n // tm,),
        in_specs=[
            row_spec(D_MODEL),
            _const_spec(w_attn.shape), _const_spec(w_dqkv.shape), _const_spec(w_dz.shape),
            _const_spec(w_ba.shape), _const_spec(conv_w.shape), _const_spec(gpar.shape),
        ],
        out_specs=(
            row_spec(attn_w), row_spec(DN_QK_W), row_spec(DN_QK_W), row_spec(DN_V_W), row_spec(DN_V_W),
            row_spec(LANES),
            pl.BlockSpec((None, SUBLANES, tm), lambda i: (i // tiles_per_seq, 0, i % tiles_per_seq)),
        ),
        out_shape=out_shape,
        scratch_shapes=[pltpu.VMEM((SUBLANES, DN_CONV_W), F32)],
        compiler_params=pltpu.CompilerParams(dimension_semantics=("arbitrary",), vmem_limit_bytes=VMEM_LIMIT),
        name="proj_dnprep",
    )(x2, w_attn, w_dqkv, w_dz, w_ba, conv_w, gpar)


def _attn_kernel(sink_ref, q_ref, kv_ref, kvp_ref, o_ref):
    j = pl.program_id(1)
    tq = q_ref.shape[0]
    w = WINDOW
    pair_w = 2 * ATTN_HEAD_DIM
    lane = lax.broadcasted_iota(jnp.int32, (2 * w, pair_w), 1)
    lo = lane < ATTN_HEAD_DIM
    qi = lax.broadcasted_iota(jnp.int32, (w, 2 * w), 0)
    kj = lax.broadcasted_iota(jnp.int32, (w, 2 * w), 1)
    band = (kj > qi) & (kj <= qi + w)
    lane_o = lax.broadcasted_iota(jnp.int32, (w, pair_w), 1)
    lo_o = lane_o < ATTN_HEAD_DIM
    zero = jnp.zeros((), BF16)

    for wi in range(tq // w):
        if wi == 0:
            kv_prev = kvp_ref[...]
            mask = band & ((kj >= w) | (j > 0))
        else:
            kv_prev = kv_ref[(wi - 1) * w:wi * w, :]
            mask = band
        kv2 = jnp.concatenate([kv_prev, kv_ref[wi * w:(wi + 1) * w, :]], axis=0)
        q = q_ref[wi * w:(wi + 1) * w, :]
        for hk in range(ATTN_KV_HEADS):
            kd = kv2[:, hk * pair_w:(hk + 1) * pair_w]
            vd = kv2[:, (ATTN_KV_HEADS + hk) * pair_w:(ATTN_KV_HEADS + hk + 1) * pair_w]
            kbd = jnp.concatenate([jnp.where(lo, kd, zero), jnp.where(lo, zero, kd)], axis=0)
            vbd = jnp.concatenate([jnp.where(lo, vd, zero), jnp.where(lo, zero, vd)], axis=0)
            for pp in range(ATTN_HEADS // ATTN_KV_HEADS // 2):
                pair = hk * (ATTN_HEADS // ATTN_KV_HEADS // 2) + pp
                qp = q[:, pair * pair_w:(pair + 1) * pair_w]
                s = _dot_nt(qp, kbd)
                ps, invs = [], []
                for e in range(2):
                    se = jnp.where(mask, s[:, e * 2 * w:(e + 1) * 2 * w], -1e30)
                    sink = sink_ref[2 * pair + e]
                    m = jnp.maximum(jnp.max(se, axis=-1, keepdims=True), sink)
                    pe = jnp.exp(se - m)
                    den = jnp.sum(pe, axis=-1, keepdims=True) + jnp.exp(sink - m)
                    ps.append(pe.astype(BF16))
                    invs.append(1.0 / den)
                o = _dot(jnp.concatenate(ps, axis=1), vbd)
                o = o * jnp.where(lo_o, invs[0], invs[1])
                o_ref[wi * w:(wi + 1) * w, pair * pair_w:(pair + 1) * pair_w] = o.astype(BF16)


def _attention(aqkv, sinks, batch, seq):
    n = aqkv.shape[0]
    tq = ATTN_TQ
    nq = seq // tq
    wpt = tq // WINDOW
    kv_w = aqkv.shape[1] - ATTN_Q_W
    assert kv_w == ATTN_Q_W
    return pl.pallas_call(
        _attn_kernel,
        grid=(batch, nq),
        in_specs=[
            pl.BlockSpec(memory_space=pltpu.SMEM),
            pl.BlockSpec((tq, ATTN_Q_W), lambda b, j: (b * nq + j, 0)),
            pl.BlockSpec((tq, kv_w), lambda b, j: (b * nq + j, 1)),
            pl.BlockSpec((WINDOW, kv_w), lambda b, j: (jnp.maximum((b * nq + j) * wpt - 1, 0), 1)),
        ],
        out_specs=pl.BlockSpec((tq, ATTN_Q_W), lambda b, j: (b * nq + j, 0)),
        out_shape=jax.ShapeDtypeStruct((n, ATTN_Q_W), BF16),
        compiler_params=pltpu.CompilerParams(dimension_semantics=("arbitrary", "arbitrary"),
                                             vmem_limit_bytes=VMEM_LIMIT),
        name="swa_sinks",
    )(sinks, aqkv, aqkv, aqkv)


def _chunk_gates(gbc_ref, gbr_ref, b, h, r0):
    c = CHUNK
    beta_c = gbc_ref[b, r0:r0 + c, h:h + 1]
    g_c = gbc_ref[b, r0:r0 + c, DN_HEADS + h:DN_HEADS + h + 1]
    beta_r = gbr_ref[b, h:h + 1, r0:r0 + c]
    g_r = gbr_ref[b, DN_HEADS + h:DN_HEADS + h + 1, r0:r0 + c]
    return beta_c, g_c, beta_r, g_r


def _delta_prep_kernel(q_ref, k_ref, v_ref, gbc_ref, gbr_ref, u_ref, w_ref, attn_ref, mat_ref, at_ref, x_ref):
    batch = q_ref.shape[0]
    c = CHUNK
    n_chunks = q_ref.shape[1] // c
    items = [(cc, b, h) for cc in range(n_chunks) for b in range(batch) for h in range(DN_HEADS)]
    assert len(items) == LANES and c * 2 == LANES

    row = lax.broadcasted_iota(jnp.int32, (c, LANES), 0)
    col = lax.broadcasted_iota(jnp.int32, (c, LANES), 1)
    causal = row >= col
    strict = row > col

    for m, (cc, b, h) in enumerate(items):
        r0, lo = cc * c, h * DN_KEY_DIM
        k = k_ref[b, r0:r0 + c, lo:lo + DN_KEY_DIM]
        kq = jnp.concatenate([q_ref[b, r0:r0 + c, lo:lo + DN_KEY_DIM], k], axis=0)
        s = _dot_nt(kq, jnp.concatenate([k, k], axis=0))
        beta_c, g_c, _, g_r = _chunk_gates(gbc_ref, gbr_ref, b, h, r0)
        g_r2 = jnp.concatenate([g_r, g_r], axis=1)
        decay = jnp.exp(jnp.where(causal, g_c - g_r2, -1e30))
        attn_ref[b, h, r0:r0 + c, :] = (s[:c] * decay)[:, :c].astype(BF16)
        mat_ref[m * c:(m + 1) * c, :] = jnp.where(strict, beta_c * s[c:] * decay, 0.0)

    for i in range(1, c):
        at_ref[i] = mat_ref[pl.ds(i, LANES, stride=c), :].T[:c, :]

    sub = lax.broadcasted_iota(jnp.int32, (SUBLANES, LANES), 0)
    zero_blk = jnp.zeros((SUBLANES, LANES), F32)
    for i in range(c):
        nblk = i // SUBLANES + 1
        acc = [zero_blk] * (nblk - 1) + [(sub == i % SUBLANES).astype(F32)]
        for j in range(i):
            a_ij = jnp.broadcast_to(at_ref[i, j:j + 1, :], (SUBLANES, LANES))
            for blk in range(j // SUBLANES + 1):
                acc[blk] = acc[blk] - a_ij * x_ref[j, blk * SUBLANES:(blk + 1) * SUBLANES, :]
        for blk in range(c // SUBLANES):
            x_ref[i, blk * SUBLANES:(blk + 1) * SUBLANES, :] = acc[blk] if blk < nblk else zero_blk

    zeros_half = jnp.zeros((LANES - c, LANES), F32)
    for i in range(c):
        mat_ref[pl.ds(i, LANES, stride=c), :] = jnp.concatenate([x_ref[i], zeros_half], axis=0).T

    for m, (cc, b, h) in enumerate(items):
        r0, lo = cc * c, h * DN_KEY_DIM
        _, _, beta_r, g_r = _chunk_gates(gbc_ref, gbr_ref, b, h, r0)
        t = mat_ref[m * c:(m + 1) * c, :][:, :c]
        u_ref[b, r0:r0 + c, lo:lo + DN_VALUE_DIM] = _dot((t * beta_r).astype(BF16), v_ref[b, r0:r0 + c, lo:lo + DN_VALUE_DIM])
        w = _dot((t * (beta_r * jnp.exp(g_r))).astype(BF16), k_ref[b, r0:r0 + c, lo:lo + DN_KEY_DIM])
        w_ref[b, r0:r0 + c, lo:lo + DN_KEY_DIM] = w.astype(BF16)


def _delta_prep(dq, dk, dv, gbc, gbr, batch, seq):
    ts = DN_PREP_STEP
    tok_spec = lambda width: pl.BlockSpec((batch, ts, width), lambda i: (0, i, 0))
    return pl.pallas_call(
        _delta_prep_kernel,
        grid=(seq // ts,),
        in_specs=[
            tok_spec(DN_QK_W), tok_spec(DN_QK_W), tok_spec(DN_V_W), tok_spec(LANES),
            pl.BlockSpec((batch, SUBLANES, ts), lambda i: (0, 0, i)),
        ],
        out_specs=(
            tok_spec(DN_V_W), tok_spec(DN_QK_W),
            pl.BlockSpec((batch, DN_HEADS, ts, CHUNK), lambda i: (0, 0, i, 0)),
        ),
        out_shape=(
            jax.ShapeDtypeStruct((batch, seq, DN_V_W), F32),
            jax.ShapeDtypeStruct((batch, seq, DN_QK_W), BF16),
            jax.ShapeDtypeStruct((batch, DN_HEADS, seq, CHUNK), BF16),
        ),
        scratch_shapes=[
            pltpu.VMEM((LANES * CHUNK, LANES), F32),
            pltpu.VMEM((CHUNK, CHUNK, LANES), F32),
            pltpu.VMEM((CHUNK, CHUNK, LANES), F32),
        ],
        compiler_params=pltpu.CompilerParams(dimension_semantics=("arbitrary",), vmem_limit_bytes=VMEM_LIMIT),
        name="delta_prep",
    )(dq, dk, dv, gbc, gbr)


def _delta_scan_kernel(q_ref, k_ref, u_ref, w_ref, attn_ref, z_ref, gbc_ref, nw_ref, o_ref, state_ref):
    step = pl.program_id(0)
    batch = q_ref.shape[0]
    c = CHUNK

    @pl.when(step == 0)
    def _():
        state_ref[...] = jnp.zeros_like(state_ref)

    nw = nw_ref[...]
    chains = [(b, h) for b in range(batch) for h in range(DN_HEADS)]

    for cc in range(q_ref.shape[1] // c):
        r0 = cc * c
        ws = {}
        for (b, h) in chains:
            lo = h * DN_KEY_DIM
            g_c = gbc_ref[b, r0:r0 + c, DN_HEADS + h:DN_HEADS + h + 1]
            qg = (q_ref[b, r0:r0 + c, lo:lo + DN_KEY_DIM].astype(F32) * jnp.exp(g_c)).astype(BF16)
            lhs = jnp.concatenate([w_ref[b, r0:r0 + c, lo:lo + DN_KEY_DIM], qg], axis=0)
            ws[(b, h)] = _dot(lhs, state_ref[b * DN_HEADS + h].astype(BF16))
        for (b, h) in chains:
            lo = h * DN_KEY_DIM
            idx = b * DN_HEADS + h
            g_c = gbc_ref[b, r0:r0 + c, DN_HEADS + h:DN_HEADS + h + 1]
            g_last = g_c[c - 1:c, :]
            res = ws[(b, h)]
            vb = (u_ref[b, r0:r0 + c, lo:lo + DN_VALUE_DIM] - res[:c]).astype(BF16)
            o = res[c:] + _dot(attn_ref[b, h, r0:r0 + c, :], vb)
            k_dec = (k_ref[b, r0:r0 + c, lo:lo + DN_KEY_DIM].astype(F32) * jnp.exp(g_last - g_c)).astype(BF16)
            state_ref[idx] = state_ref[idx] * jnp.exp(g_last) + _dot_tn(k_dec, vb)
            o = o * lax.rsqrt(jnp.mean(o * o, axis=-1, keepdims=True) + RMS_EPS) * nw
            o = o * z_ref[b, r0:r0 + c, lo:lo + DN_VALUE_DIM].astype(F32)
            o_ref[b, r0:r0 + c, lo:lo + DN_VALUE_DIM] = o.astype(BF16)


def _delta_scan(dq, dk, u, w, attn, dzs, gbc, norm_w, batch, seq):
    ts = DN_STEP
    tok_spec = lambda width: pl.BlockSpec((batch, ts, width), lambda i: (0, i, 0))
    return pl.pallas_call(
        _delta_scan_kernel,
        grid=(seq // ts,),
        in_specs=[
            tok_spec(DN_QK_W), tok_spec(DN_QK_W), tok_spec(DN_V_W), tok_spec(DN_QK_W),
            pl.BlockSpec((batch, DN_HEADS, ts, CHUNK), lambda i: (0, 0, i, 0)),
            tok_spec(DN_V_W), tok_spec(LANES),
            _const_spec(norm_w.shape),
        ],
        out_specs=tok_spec(DN_V_W),
        out_shape=jax.ShapeDtypeStruct((batch, seq, DN_V_W), BF16),
        scratch_shapes=[pltpu.VMEM((batch * DN_HEADS, DN_KEY_DIM, DN_VALUE_DIM), F32)],
        compiler_params=pltpu.CompilerParams(dimension_semantics=("arbitrary",), vmem_limit_bytes=VMEM_LIMIT),
        name="delta_scan",
    )(dq, dk, u, w, attn, dzs, gbc, norm_w)


def _layer_norm(x, g, b):
    mu = jnp.mean(x, axis=-1, keepdims=True)
    xc = x - mu
    var = jnp.mean(xc * xc, axis=-1, keepdims=True)
    return xc * lax.rsqrt(var + LN_EPS) * g + b


def _merge_ffn_kernel(x_ref, oa_ref, ob_ref, wg_ref, wao_ref, wdo_ref, wout_ref, ln1_ref,
                      wup_ref, wdown_ref, ln2_ref, out_ref):
    x = x_ref[...]
    gates = _dot(x.astype(BF16), wg_ref[...])
    ya = _dot(oa_ref[...], wao_ref[...])
    yb = _dot(ob_ref[...], wdo_ref[...])
    merged = _sigmoid(gates[:, :D_MODEL]) * ya + _sigmoid(gates[:, D_MODEL:]) * yb
    mix = _dot(merged.astype(BF16), wout_ref[...])
    ln1 = ln1_ref[...]
    x1 = _layer_norm(DEEPNORM_ALPHA * x + mix, ln1[0:1, :], ln1[1:2, :])
    h = jnp.maximum(_dot(x1.astype(BF16), wup_ref[...]), 0.0)
    h = _dot((h * h).astype(BF16), wdown_ref[...])
    ln2 = ln2_ref[...]
    out_ref[...] = _layer_norm(DEEPNORM_ALPHA * x1 + h, ln2[0:1, :], ln2[1:2, :])


def _merge_ffn(x2, oa, ob, wg, wao, wdo, wout, ln1, wup, wdown, ln2):
    n = x2.shape[0]
    tm = FFN_TM
    row_spec = lambda width: pl.BlockSpec((tm, width), lambda i: (i, 0))
    return pl.pallas_call(
        _merge_ffn_kernel,
        grid=(n // tm,),
        in_specs=[
            row_spec(D_MODEL), row_spec(ATTN_Q_W), row_spec(DN_V_W),
            _const_spec(wg.shape), _const_spec(wao.shape), _const_spec(wdo.shape), _const_spec(wout.shape),
            _const_spec(ln1.shape), _const_spec(wup.shape), _const_spec(wdown.shape), _const_spec(ln2.shape),
        ],
        out_specs=row_spec(D_MODEL),
        out_shape=jax.ShapeDtypeStruct((n, D_MODEL), F32),
        compiler_params=pltpu.CompilerParams(dimension_semantics=("arbitrary",), vmem_limit_bytes=VMEM_LIMIT),
        name="merge_ffn",
    )(x2, oa, ob, wg, wao, wdo, wout, ln1, wup, wdown, ln2)


def _layer(x, w_in, conv_w, attn_sinks, dn_a_log, dn_dt_bias, dn_norm_w, w_attn_out, w_dn_out, w_out,
           ln1_g, ln1_b, w_up, w_down, ln2_g, ln2_b):
    batch, seq, _ = x.shape
    n = batch * seq
    x2 = x.reshape(n, D_MODEL)

    c0 = ATTN_Q_W
    c1 = c0 + ATTN_KV_W
    c2 = c1 + ATTN_KV_W
    c3 = c2 + DN_CONV_W
    c4 = c3 + DN_V_W
    c5 = c4 + 2 * DN_HEADS
    wb = w_in.astype(BF16)
    hd = ATTN_HEAD_DIM
    dup = lambda wcols: jnp.concatenate(
        [wcols[:, (hh // 2) * hd:(hh // 2 + 1) * hd] for hh in range(2 * ATTN_KV_HEADS)], axis=1)
    w_attn = jnp.concatenate([wb[:, :c0], dup(wb[:, c0:c1]), dup(wb[:, c1:c2])], axis=1)
    w_dqkv = wb[:, c2:c3]
    w_dz = wb[:, c3:c4]
    w_ba = jnp.pad(wb[:, c4:c5], ((0, 0), (0, LANES - 2 * DN_HEADS)))
    w_gates = wb[:, c5:]
    gpar = jnp.zeros((2, LANES), F32)
    gpar = gpar.at[0, DN_HEADS:2 * DN_HEADS].set(dn_a_log.astype(F32))
    gpar = gpar.at[1, DN_HEADS:2 * DN_HEADS].set(dn_dt_bias.astype(F32))

    aqkv, dq, dk, dv, dzs, gbc, gbr = _project(x2, w_attn, w_dqkv, w_dz, w_ba, conv_w.astype(F32), gpar, batch, seq)
    oa = _attention(aqkv, attn_sinks.astype(F32), batch, seq)
    r3 = lambda t: t.reshape(batch, seq, t.shape[-1])
    dq, dk, gbc = r3(dq), r3(dk), r3(gbc)
    u, w, attn = _delta_prep(dq, dk, r3(dv), gbc, gbr, batch, seq)
    ob = _delta_scan(dq, dk, u, w, attn, r3(dzs), gbc,
                     dn_norm_w.astype(F32).reshape(1, DN_VALUE_DIM), batch, seq)
    out = _merge_ffn(
        x2, oa, ob.reshape(n, DN_V_W), w_gates, w_attn_out.astype(BF16), w_dn_out.astype(BF16),
        w_out.astype(BF16), jnp.stack([ln1_g, ln1_b]).astype(F32), w_up.astype(BF16), w_down.astype(BF16),
        jnp.stack([ln2_g, ln2_b]).astype(F32))
    return out.reshape(batch, seq, D_MODEL)


def kernel(x, w_in, conv_w, attn_sinks, dn_a_log, dn_dt_bias, dn_norm_w, w_attn_out, w_dn_out, w_out,
           ln1_g, ln1_b, w_up, w_down, ln2_g, ln2_b):
    for l in range(DEPTH):
        x = _layer(x, w_in[l], conv_w[l], attn_sinks[l], dn_a_log[l], dn_dt_bias[l], dn_norm_w[l],
                   w_attn_out[l], w_dn_out[l], w_out[l], ln1_g[l], ln1_b[l], w_up[l], w_down[l],
                   ln2_g[l], ln2_b[l])
    return x
```

```python
import functools

import jax
import jax.numpy as jnp
from jax import lax
from jax.experimental import pallas as pl
from jax.experimental.pallas import tpu as pltpu

D_MODEL = 1024
ATTN_HEADS = 8
ATTN_KV_HEADS = 2
ATTN_HEAD_DIM = 64
WINDOW = 128
DN_HEADS = 4
DN_KEY_DIM = 128
DN_VALUE_DIM = 128
CONV_WIDTH = 4
CHUNK = 64
D_FF = 4 * D_MODEL
LN_EPS = 1e-5
RMS_EPS = 1e-6
DEPTH = 1
DEEPNORM_ALPHA = (2 * DEPTH) ** 0.25

ATTN_Q_W = ATTN_HEADS * ATTN_HEAD_DIM
ATTN_KV_W = ATTN_KV_HEADS * ATTN_HEAD_DIM
DN_QK_W = DN_HEADS * DN_KEY_DIM
DN_V_W = DN_HEADS * DN_VALUE_DIM
DN_CONV_W = 2 * DN_QK_W + DN_V_W

LANES = 128
SUBLANES = 8
VMEM_LIMIT = 56 * 1024 * 1024

PROJ_TM = 512
PROJ_CONV_GROUP = 2 * DN_KEY_DIM
ATTN_TQ = 512
DN_PREP_STEP = 512
DN_STEP = 128
FFN_TM = 512
FFN_CHUNK = 1024

BF16 = jnp.bfloat16
F32 = jnp.float32


def _dot(a, b):
    return jnp.dot(a, b, preferred_element_type=F32)


def _dot_nt(a, b):
    return lax.dot_general(a, b, (((1,), (1,)), ((), ())), preferred_element_type=F32)


def _dot_tn(a, b):
    return lax.dot_general(a, b, (((0,), (0,)), ((), ())), preferred_element_type=F32)


def _sigmoid(x):
    return 0.5 + 0.5 * jnp.tanh(0.5 * x)


def _const_spec(shape):
    return pl.BlockSpec(shape, lambda *_: (0,) * len(shape), pipeline_mode=pl.Buffered(1))


def _proj_kernel(x_ref, w_attn_ref, w_dqkv_ref, w_dz_ref, w_ba_ref, conv_ref, gpar_ref,
                 aqkv_ref, dq_ref, dk_ref, dkt_ref, dv_ref, dzs_ref, gbc_ref, gbr_ref,
                 *pbuf_refs, tiles_per_seq):
    i = pl.program_id(0)
    tm = x_ref.shape[0]
    xb = x_ref[...].astype(BF16)
    first = i % tiles_per_seq == 0
    half_cw = 0.5 * conv_ref[...]
    grp = PROJ_CONV_GROUP

    @pl.when(i == 0)
    def _():
        for pbuf_ref in pbuf_refs:
            pbuf_ref[tm:tm + SUBLANES, :] = jnp.zeros((SUBLANES, grp), F32)

    def silu_of_twice(h):
        return h + h * jnp.tanh(h)

    def conv_matmul(gi):
        pbuf_ref = pbuf_refs[gi]
        carry = pbuf_ref[tm:tm + SUBLANES, :]
        pbuf_ref[0:SUBLANES, :] = jnp.where(first, 0.0, carry)
        pbuf_ref[SUBLANES:SUBLANES + tm, :] = _dot(xb, w_dqkv_ref[:, gi * grp:(gi + 1) * grp])

    def conv_silu(gi):
        h = None
        for s in range(CONV_WIDTH):
            taps = half_cw[CONV_WIDTH - 1 - s:CONV_WIDTH - s, gi * grp:(gi + 1) * grp]
            term = pbuf_refs[gi][pl.ds(SUBLANES - s, tm), :] * taps
            h = term if h is None else h + term
        return silu_of_twice(h)

    def unit_rows(yh, scale):
        return yh * (lax.rsqrt(jnp.sum(yh * yh, axis=-1, keepdims=True) + RMS_EPS) * scale)

    def q_tail(c0, _):
        y = conv_silu(c0 // grp)
        for lo in range(0, grp, DN_KEY_DIM):
            dq_ref[:, c0 + lo:c0 + lo + DN_KEY_DIM] = unit_rows(y[:, lo:lo + DN_KEY_DIM], DN_KEY_DIM ** -0.5).astype(BF16)

    def k_tail(c0, _):
        y = conv_silu((DN_QK_W + c0) // grp)
        for lo in range(0, grp, DN_KEY_DIM):
            kh = unit_rows(y[:, lo:lo + DN_KEY_DIM], 1.0).astype(BF16)
            dk_ref[:, c0 + lo:c0 + lo + DN_KEY_DIM] = kh
            dkt_ref[c0 + lo:c0 + lo + DN_KEY_DIM, :] = kh.astype(F32).T.astype(BF16)

    def v_tail(c0, _):
        dv_ref[:, c0:c0 + grp] = conv_silu((2 * DN_QK_W + c0) // grp).astype(BF16)

    def attn_tail(c0, a):
        aqkv_ref[:, c0:c0 + ATTN_Q_W] = a.astype(BF16)

    def dz_tail(_, a):
        dzs_ref[...] = silu_of_twice(0.5 * a).astype(BF16)

    def gate_tail(_, ba):
        gpar = gpar_ref[...]
        lane = lax.broadcasted_iota(jnp.int32, (tm, LANES), 1)
        row = lax.broadcasted_iota(jnp.int32, (tm, LANES), 0)
        beta = _sigmoid(ba)
        z = ba + gpar[1:2, :]
        softplus = jnp.maximum(z, 0.0) + jnp.log1p(jnp.exp(-jnp.abs(z)))
        g = -jnp.exp(gpar[0:1, :]) * softplus
        s = 1
        while s < CHUNK:
            g = g + jnp.where(row % CHUNK >= s, pltpu.roll(g, s, 0), 0.0)
            s *= 2
        gb = jnp.where(lane < DN_HEADS, beta, jnp.where(lane < 2 * DN_HEADS, g, 0.0))
        gbc_ref[...] = gb
        gbr_ref[...] = gb.T[:SUBLANES, :]

    conv_stages = []
    for c0 in range(0, DN_QK_W, grp):
        conv_stages.append((functools.partial(conv_matmul, c0 // grp), q_tail, c0))
    for c0 in range(0, DN_QK_W, grp):
        conv_stages.append((functools.partial(conv_matmul, (DN_QK_W + c0) // grp), k_tail, c0))
    for c0 in range(0, DN_V_W, grp):
        conv_stages.append((functools.partial(conv_matmul, (2 * DN_QK_W + c0) // grp), v_tail, c0))
    plain_stages = []
    for c0 in range(0, w_attn_ref.shape[1], ATTN_Q_W):
        plain_stages.append((functools.partial(lambda c: _dot(xb, w_attn_ref[:, c:c + ATTN_Q_W]), c0), attn_tail, c0))
    plain_stages.append((lambda: _dot(xb, w_dz_ref[...]), dz_tail, 0))
    plain_stages.append((lambda: _dot(xb, w_ba_ref[...]), gate_tail, 0))
    stages = []
    for n, st in enumerate(conv_stages):
        stages.append(st)
        if n < len(plain_stages):
            stages.append(plain_stages[n])
    stages += plain_stages[len(conv_stages):]

    pending = stages[0][0]()
    for n, (_, tail, c0) in enumerate(stages):
        ready = pending
        if n + 1 < len(stages):
            pending = stages[n + 1][0]()
        tail(c0, ready)


def _project(x2, w_attn, w_dqkv, w_dz, w_ba, conv_w, gpar, batch, seq):
    n = x2.shape[0]
    tm = PROJ_TM
    tiles_per_seq = seq // tm
    attn_w = w_attn.shape[1]
    row_spec = lambda width: pl.BlockSpec((tm, width), lambda i: (i, 0))
    out_shape = (
        jax.ShapeDtypeStruct((n, attn_w), BF16),
        jax.ShapeDtypeStruct((n, DN_QK_W), BF16),
        jax.ShapeDtypeStruct((n, DN_QK_W), BF16),
        jax.ShapeDtypeStruct((batch, DN_QK_W, seq), BF16),
        jax.ShapeDtypeStruct((n, DN_V_W), BF16),
        jax.ShapeDtypeStruct((n, DN_V_W), BF16),
        jax.ShapeDtypeStruct((n, LANES), F32),
        jax.ShapeDtypeStruct((batch, SUBLANES, seq), F32),
    )
    seq_spec = lambda rows: pl.BlockSpec((None, rows, tm), lambda i: (i // tiles_per_seq, 0, i % tiles_per_seq))
    return pl.pallas_call(
        functools.partial(_proj_kernel, tiles_per_seq=tiles_per_seq),
        grid=(n // tm,),
        in_specs=[
            row_spec(D_MODEL),
            _const_spec(w_attn.shape), _const_spec(w_dqkv.shape), _const_spec(w_dz.shape),
            _const_spec(w_ba.shape), _const_spec(conv_w.shape), _const_spec(gpar.shape),
        ],
        out_specs=(
            row_spec(attn_w), row_spec(DN_QK_W), row_spec(DN_QK_W), seq_spec(DN_QK_W),
            row_spec(DN_V_W), row_spec(DN_V_W), row_spec(LANES), seq_spec(SUBLANES),
        ),
        out_shape=out_shape,
        scratch_shapes=[pltpu.VMEM((SUBLANES + tm, PROJ_CONV_GROUP), F32)] * (DN_CONV_W // PROJ_CONV_GROUP),
        compiler_params=pltpu.CompilerParams(dimension_semantics=("arbitrary",), vmem_limit_bytes=VMEM_LIMIT),
        name="proj_dnprep",
    )(x2, w_attn, w_dqkv, w_dz, w_ba, conv_w, gpar)


def _attn_kernel(sink_ref, q_ref, kv_ref, kvp_ref, o_ref):
    j = pl.program_id(1)
    tq = q_ref.shape[0]
    w = WINDOW
    pair_w = 2 * ATTN_HEAD_DIM
    group = ATTN_HEADS // ATTN_KV_HEADS
    lo = lax.broadcasted_iota(jnp.int32, (w, pair_w), 1) < ATTN_HEAD_DIM
    qi = lax.broadcasted_iota(jnp.int32, (w, 2 * w), 0)
    kj = lax.broadcasted_iota(jnp.int32, (w, 2 * w), 1)
    band = (kj > qi) & (kj <= qi + w)
    zero = jnp.zeros((), BF16)

    def two_windows(wi, col):
        cols = slice(col * pair_w, (col + 1) * pair_w)
        prev = kvp_ref[:, cols] if wi == 0 else kv_ref[(wi - 1) * w:wi * w, cols]
        return jnp.concatenate([prev, kv_ref[wi * w:(wi + 1) * w, cols]], axis=0)

    def scores(wi, hk):
        parts = []
        for pp in range(group // 2):
            pair = hk * (group // 2) + pp
            qp = q_ref[wi * w:(wi + 1) * w, pair * pair_w:(pair + 1) * pair_w]
            parts += [jnp.where(lo, qp, zero), jnp.where(lo, zero, qp)]
        return _dot_nt(jnp.concatenate(parts, axis=0), two_windows(wi, hk))

    def finish(wi, hk, s):
        mask = band & ((kj >= w) | (j > 0)) if wi == 0 else band
        ps, invs = [], []
        for e in range(group):
            se = jnp.where(mask, s[e * w:(e + 1) * w], -1e30)
            sink = sink_ref[hk * group + e]
            m = jnp.maximum(jnp.max(se, axis=-1, keepdims=True), sink)
            pe = jnp.exp(se - m)
            den = jnp.sum(pe, axis=-1, keepdims=True) + jnp.exp(sink - m)
            ps.append(pe.astype(BF16))
            invs.append(1.0 / den)
        o = _dot(jnp.concatenate(ps, axis=0), two_windows(wi, ATTN_KV_HEADS + hk))
        for pp in range(group // 2):
            pair = hk * (group // 2) + pp
            oa = o[(2 * pp) * w:(2 * pp + 1) * w] * invs[2 * pp]
            ob = o[(2 * pp + 1) * w:(2 * pp + 2) * w] * invs[2 * pp + 1]
            o_ref[wi * w:(wi + 1) * w, pair * pair_w:(pair + 1) * pair_w] = jnp.where(lo, oa, ob).astype(BF16)

    units = [(wi, hk) for wi in range(tq // w) for hk in range(ATTN_KV_HEADS)]
    s_next = scores(*units[0])
    for n, unit in enumerate(units):
        s = s_next
        if n + 1 < len(units):
            s_next = scores(*units[n + 1])
        finish(*unit, s)


def _attention(aqkv, sinks, batch, seq):
    n = aqkv.shape[0]
    tq = ATTN_TQ
    nq = seq // tq
    wpt = tq // WINDOW
    kv_w = aqkv.shape[1] - ATTN_Q_W
    assert kv_w == ATTN_Q_W
    return pl.pallas_call(
        _attn_kernel,
        grid=(batch, nq),
        in_specs=[
            pl.BlockSpec(memory_space=pltpu.SMEM),
            pl.BlockSpec((tq, ATTN_Q_W), lambda b, j: (b * nq + j, 0)),
            pl.BlockSpec((tq, kv_w), lambda b, j: (b * nq + j, 1)),
            pl.BlockSpec((WINDOW, kv_w), lambda b, j: (jnp.maximum((b * nq + j) * wpt - 1, 0), 1)),
        ],
        out_specs=pl.BlockSpec((tq, ATTN_Q_W), lambda b, j: (b * nq + j, 0)),
        out_shape=jax.ShapeDtypeStruct((n, ATTN_Q_W), BF16),
        compiler_params=pltpu.CompilerParams(dimension_semantics=("arbitrary", "arbitrary"),
                                             vmem_limit_bytes=VMEM_LIMIT),
        name="swa_sinks",
    )(sinks, aqkv, aqkv, aqkv)


def _chunk_gates(gbc_ref, gbr_ref, b, h, r0):
    c = CHUNK
    beta_c = gbc_ref[b, r0:r0 + c, h:h + 1]
    g_c = gbc_ref[b, r0:r0 + c, DN_HEADS + h:DN_HEADS + h + 1]
    beta_r = gbr_ref[b, h:h + 1, r0:r0 + c]
    g_r = gbr_ref[b, DN_HEADS + h:DN_HEADS + h + 1, r0:r0 + c]
    return beta_c, g_c, beta_r, g_r


def _delta_prep_kernel(q_ref, k_ref, v_ref, gbc_ref, gbr_ref, u_ref, w_ref, qg_ref, attn_ref,
                       mat_ref, at_ref, x_ref):
    batch = q_ref.shape[0]
    c = CHUNK
    n_chunks = q_ref.shape[1] // c
    items = [(cc, b, h) for cc in range(n_chunks) for b in range(batch) for h in range(DN_HEADS)]
    assert len(items) == LANES and c * 2 == LANES

    row = lax.broadcasted_iota(jnp.int32, (c, LANES), 0)
    col = lax.broadcasted_iota(jnp.int32, (c, LANES), 1)
    causal = row >= col
    strict = row > col

    for m, (cc, b, h) in enumerate(items):
        r0, lo = cc * c, h * DN_KEY_DIM
        k = k_ref[b, r0:r0 + c, lo:lo + DN_KEY_DIM]
        q = q_ref[b, r0:r0 + c, lo:lo + DN_KEY_DIM]
        s = _dot_nt(jnp.concatenate([q, k], axis=0), jnp.concatenate([k, k], axis=0))
        beta_c, g_c, _, g_r = _chunk_gates(gbc_ref, gbr_ref, b, h, r0)
        qg_ref[b, r0:r0 + c, lo:lo + DN_KEY_DIM] = (q.astype(F32) * jnp.exp(g_c)).astype(BF16)
        g_r2 = jnp.concatenate([g_r, g_r], axis=1)
        decay = jnp.exp(jnp.where(causal, g_c - g_r2, -1e30))
        attn_ref[b, h, r0:r0 + c, :] = (s[:c] * decay)[:, :c].astype(BF16)
        mat_ref[m * c:(m + 1) * c, :] = jnp.where(strict, beta_c * s[c:] * decay, 0.0)

    for i in range(1, c):
        at_ref[i] = mat_ref[pl.ds(i, LANES, stride=c), :].T[:c, :]

    sub = lax.broadcasted_iota(jnp.int32, (SUBLANES, LANES), 0)
    zero_blk = jnp.zeros((SUBLANES, LANES), F32)
    for i in range(c):
        nblk = i // SUBLANES + 1
        acc = [zero_blk] * (nblk - 1) + [(sub == i % SUBLANES).astype(F32)]
        for j in range(i):
            a_ij = jnp.broadcast_to(at_ref[i, j:j + 1, :], (SUBLANES, LANES))
            for blk in range(j // SUBLANES + 1):
                acc[blk] = acc[blk] - a_ij * x_ref[j, blk * SUBLANES:(blk + 1) * SUBLANES, :]
        for blk in range(c // SUBLANES):
            x_ref[i, blk * SUBLANES:(blk + 1) * SUBLANES, :] = acc[blk] if blk < nblk else zero_blk

    zeros_half = jnp.zeros((LANES - c, LANES), F32)
    for i in range(c):
        mat_ref[pl.ds(i, LANES, stride=c), :] = jnp.concatenate([x_ref[i], zeros_half], axis=0).T

    for m, (cc, b, h) in enumerate(items):
        r0, lo = cc * c, h * DN_KEY_DIM
        _, _, beta_r, g_r = _chunk_gates(gbc_ref, gbr_ref, b, h, r0)
        t = mat_ref[m * c:(m + 1) * c, :][:, :c]
        u_ref[b, r0:r0 + c, lo:lo + DN_VALUE_DIM] = _dot((t * beta_r).astype(BF16), v_ref[b, r0:r0 + c, lo:lo + DN_VALUE_DIM])
        w = _dot((t * (beta_r * jnp.exp(g_r))).astype(BF16), k_ref[b, r0:r0 + c, lo:lo + DN_KEY_DIM])
        w_ref[b, r0:r0 + c, lo:lo + DN_KEY_DIM] = w.astype(BF16)


def _delta_prep(dq, dk, dv, gbc, gbr, batch, seq):
    ts = DN_PREP_STEP
    tok_spec = lambda width: pl.BlockSpec((batch, ts, width), lambda i: (0, i, 0))
    return pl.pallas_call(
        _delta_prep_kernel,
        grid=(seq // ts,),
        in_specs=[
            tok_spec(DN_QK_W), tok_spec(DN_QK_W), tok_spec(DN_V_W), tok_spec(LANES),
            pl.BlockSpec((batch, SUBLANES, ts), lambda i: (0, 0, i)),
        ],
        out_specs=(
            tok_spec(DN_V_W), tok_spec(DN_QK_W), tok_spec(DN_QK_W),
            pl.BlockSpec((batch, DN_HEADS, ts, CHUNK), lambda i: (0, 0, i, 0)),
        ),
        out_shape=(
            jax.ShapeDtypeStruct((batch, seq, DN_V_W), F32),
            jax.ShapeDtypeStruct((batch, seq, DN_QK_W), BF16),
            jax.ShapeDtypeStruct((batch, seq, DN_QK_W), BF16),
            jax.ShapeDtypeStruct((batch, DN_HEADS, seq, CHUNK), BF16),
        ),
        scratch_shapes=[
            pltpu.VMEM((LANES * CHUNK, LANES), F32),
            pltpu.VMEM((CHUNK, CHUNK, LANES), F32),
            pltpu.VMEM((CHUNK, CHUNK, LANES), F32),
        ],
        compiler_params=pltpu.CompilerParams(dimension_semantics=("arbitrary",), vmem_limit_bytes=VMEM_LIMIT),
        name="delta_prep",
    )(dq, dk, dv, gbc, gbr)


def _delta_scan_kernel(qg_ref, kt_ref, u_ref, w_ref, attn_ref, z_ref, gbr_ref, nw_ref, o_ref, state_ref):
    step = pl.program_id(0)
    batch = qg_ref.shape[0]
    c = CHUNK
    ts = qg_ref.shape[1]
    assert ts == LANES

    @pl.when(step == 0)
    def _():
        state_ref[...] = jnp.zeros_like(state_ref)

    nw = nw_ref[...]
    chains = [(b, h) for b in range(batch) for h in range(DN_HEADS)]
    lane = lax.broadcasted_iota(jnp.int32, (1, ts), 1)

    k_dec_t, s_decay = {}, {}
    for (b, h) in chains:
        kt = kt_ref[b, h * DN_KEY_DIM:(h + 1) * DN_KEY_DIM, :].astype(F32)
        g_r = gbr_ref[b, DN_HEADS + h:DN_HEADS + h + 1, :]
        for cc in range(ts // c):
            g_last = g_r[:, (cc + 1) * c - 1:(cc + 1) * c]
            in_chunk = (lane >= cc * c) & (lane < (cc + 1) * c)
            k_dec_t[(cc, b, h)] = (kt * jnp.exp(jnp.where(in_chunk, g_last - g_r, -1e30))).astype(BF16)
            s_decay[(cc, b, h)] = jnp.exp(g_last)

    for cc in range(ts // c):
        r0 = cc * c
        res, vb, om, kv = {}, {}, {}, {}
        for (b, h) in chains:
            lo = h * DN_KEY_DIM
            lhs = jnp.concatenate([w_ref[b, r0:r0 + c, lo:lo + DN_KEY_DIM],
                                   qg_ref[b, r0:r0 + c, lo:lo + DN_KEY_DIM]], axis=0)
            res[(b, h)] = _dot(lhs, state_ref[b * DN_HEADS + h].astype(BF16))
        for (b, h) in chains:
            lo = h * DN_VALUE_DIM
            vb[(b, h)] = (u_ref[b, r0:r0 + c, lo:lo + DN_VALUE_DIM] - res[(b, h)][:c]).astype(BF16)
        for (b, h) in chains:
            om[(b, h)] = _dot(attn_ref[b, h, r0:r0 + c, :], vb[(b, h)])
            kv[(b, h)] = _dot(k_dec_t[(cc, b, h)], jnp.concatenate([vb[(b, h)]] * (ts // c), axis=0))
        for (b, h) in chains:
            idx = b * DN_HEADS + h
            state_ref[idx] = state_ref[idx] * s_decay[(cc, b, h)] + kv[(b, h)]
        for (b, h) in chains:
            lo = h * DN_VALUE_DIM
            o = res[(b, h)][c:] + om[(b, h)]
            o = o * lax.rsqrt(jnp.mean(o * o, axis=-1, keepdims=True) + RMS_EPS) * nw
            o = o * z_ref[b, r0:r0 + c, lo:lo + DN_VALUE_DIM].astype(F32)
            o_ref[b, r0:r0 + c, lo:lo + DN_VALUE_DIM] = o.astype(BF16)


def _delta_scan(qg, dkt, u, w, attn, dzs, gbr, norm_w, batch, seq):
    ts = DN_STEP
    tok_spec = lambda width: pl.BlockSpec((batch, ts, width), lambda i: (0, i, 0))
    seq_spec = lambda rows: pl.BlockSpec((batch, rows, ts), lambda i: (0, 0, i))
    return pl.pallas_call(
        _delta_scan_kernel,
        grid=(seq // ts,),
        in_specs=[
            tok_spec(DN_QK_W), seq_spec(DN_QK_W), tok_spec(DN_V_W), tok_spec(DN_QK_W),
            pl.BlockSpec((batch, DN_HEADS, ts, CHUNK), lambda i: (0, 0, i, 0)),
            tok_spec(DN_V_W), seq_spec(SUBLANES),
            _const_spec(norm_w.shape),
        ],
        out_specs=tok_spec(DN_V_W),
        out_shape=jax.ShapeDtypeStruct((batch, seq, DN_V_W), BF16),
        scratch_shapes=[pltpu.VMEM((batch * DN_HEADS, DN_KEY_DIM, DN_VALUE_DIM), F32)],
        compiler_params=pltpu.CompilerParams(dimension_semantics=("arbitrary",), vmem_limit_bytes=VMEM_LIMIT),
        name="delta_scan",
    )(qg, dkt, u, w, attn, dzs, gbr, norm_w)


def _layer_norm(x, g, b):
    mu = jnp.mean(x, axis=-1, keepdims=True)
    xc = x - mu
    var = jnp.mean(xc * xc, axis=-1, keepdims=True)
    return xc * lax.rsqrt(var + LN_EPS) * g + b


def _merge_ffn_kernel(x_ref, oa_ref, ob_ref, wg_ref, wao_ref, wdo_ref, wout_ref, ln1_ref,
                      wup_ref, wdown_ref, ln2_ref, out_ref):
    x = x_ref[...]
    xb = x.astype(BF16)
    merged = _sigmoid(_dot(xb, wg_ref[:, :D_MODEL])) * _dot(oa_ref[...], wao_ref[...])
    merged = merged + _sigmoid(_dot(xb, wg_ref[:, D_MODEL:])) * _dot(ob_ref[...], wdo_ref[...])
    mix = _dot(merged.astype(BF16), wout_ref[...])
    ln1 = ln1_ref[...]
    x1 = _layer_norm(DEEPNORM_ALPHA * x + mix, ln1[0:1, :], ln1[1:2, :])
    x1b = x1.astype(BF16)
    ffn = None
    for c0 in range(0, D_FF, FFN_CHUNK):
        h = jnp.maximum(_dot(x1b, wup_ref[:, c0:c0 + FFN_CHUNK]), 0.0)
        part = _dot((h * h).astype(BF16), wdown_ref[c0:c0 + FFN_CHUNK, :])
        ffn = part if ffn is None else ffn + part
    ln2 = ln2_ref[...]
    out_ref[...] = _layer_norm(DEEPNORM_ALPHA * x1 + ffn, ln2[0:1, :], ln2[1:2, :])


def _merge_ffn(x2, oa, ob, wg, wao, wdo, wout, ln1, wup, wdown, ln2):
    n = x2.shape[0]
    tm = FFN_TM
    row_spec = lambda width: pl.BlockSpec((tm, width), lambda i: (i, 0))
    return pl.pallas_call(
        _merge_ffn_kernel,
        grid=(n // tm,),
        in_specs=[
            row_spec(D_MODEL), row_spec(ATTN_Q_W), row_spec(DN_V_W),
            _const_spec(wg.shape), _const_spec(wao.shape), _const_spec(wdo.shape), _const_spec(wout.shape),
            _const_spec(ln1.shape), _const_spec(wup.shape), _const_spec(wdown.shape), _const_spec(ln2.shape),
        ],
        out_specs=row_spec(D_MODEL),
        out_shape=jax.ShapeDtypeStruct((n, D_MODEL), F32),
        compiler_params=pltpu.CompilerParams(dimension_semantics=("arbitrary",), vmem_limit_bytes=VMEM_LIMIT),
        name="merge_ffn",
    )(x2, oa, ob, wg, wao, wdo, wout, ln1, wup, wdown, ln2)


def _layer(x, w_in, conv_w, attn_sinks, dn_a_log, dn_dt_bias, dn_norm_w, w_attn_out, w_dn_out, w_out,
           ln1_g, ln1_b, w_up, w_down, ln2_g, ln2_b):
    batch, seq, _ = x.shape
    n = batch * seq
    x2 = x.reshape(n, D_MODEL)

    c0 = ATTN_Q_W
    c1 = c0 + ATTN_KV_W
    c2 = c1 + ATTN_KV_W
    c3 = c2 + DN_CONV_W
    c4 = c3 + DN_V_W
    c5 = c4 + 2 * DN_HEADS
    wb = w_in.astype(BF16)
    hd = ATTN_HEAD_DIM
    dup = lambda wcols: jnp.concatenate(
        [wcols[:, (hh // 2) * hd:(hh // 2 + 1) * hd] for hh in range(2 * ATTN_KV_HEADS)], axis=1)
    q_scale = jnp.asarray(ATTN_HEAD_DIM ** -0.5, BF16)
    w_attn = jnp.concatenate([wb[:, :c0] * q_scale, dup(wb[:, c0:c1]), dup(wb[:, c1:c2])], axis=1)
    w_dqkv = wb[:, c2:c3]
    w_dz = wb[:, c3:c4]
    w_ba = jnp.pad(wb[:, c4:c5], ((0, 0), (0, LANES - 2 * DN_HEADS)))
    w_gates = wb[:, c5:]
    gpar = jnp.zeros((2, LANES), F32)
    gpar = gpar.at[0, DN_HEADS:2 * DN_HEADS].set(dn_a_log.astype(F32))
    gpar = gpar.at[1, DN_HEADS:2 * DN_HEADS].set(dn_dt_bias.astype(F32))

    aqkv, dq, dk, dkt, dv, dzs, gbc, gbr = _project(
        x2, w_attn, w_dqkv, w_dz, w_ba, conv_w.astype(F32), gpar, batch, seq)
    oa = _attention(aqkv, attn_sinks.astype(F32), batch, seq)
    r3 = lambda t: t.reshape(batch, seq, t.shape[-1])
    u, w, qg, attn = _delta_prep(r3(dq), r3(dk), r3(dv), r3(gbc), gbr, batch, seq)
    ob = _delta_scan(qg, dkt, u, w, attn, r3(dzs), gbr,
                     dn_norm_w.astype(F32).reshape(1, DN_VALUE_DIM), batch, seq)
    out = _merge_ffn(
        x2, oa, ob.reshape(n, DN_V_W), w_gates, w_attn_out.astype(BF16), w_dn_out.astype(BF16),
        w_out.astype(BF16), jnp.stack([ln1_g, ln1_b]).astype(F32), w_up.astype(BF16), w_down.astype(BF16),
        jnp.stack([ln2_g, ln2_b]).astype(F32))
    return out.reshape(batch, seq, D_MODEL)


def kernel(x, w_in, conv_w, attn_sinks, dn_a_log, dn_dt_bias, dn_norm_w, w_attn_out, w_dn_out, w_out,
           ln1_g, ln1_b, w_up, w_down, ln2_g, ln2_b):
    for l in range(DEPTH):
        x = _layer(x, w_in[l], conv_w[l], attn_sinks[l], dn_a_log[l], dn_dt_bias[l], dn_norm_w[l],
                   w_attn_out[l], w_dn_out[l], w_out[l], ln1_g[l], ln1_b[l], w_up[l], w_down[l],
                   ln2_g[l], ln2_b[l])
    return x
```

```python
import functools

import jax
import jax.numpy as jnp
from jax import lax
from jax.experimental import pallas as pl
from jax.experimental.pallas import tpu as pltpu

D_MODEL = 1024
ATTN_HEADS = 8
ATTN_KV_HEADS = 2
ATTN_HEAD_DIM = 64
WINDOW = 128
DN_HEADS = 4
DN_KEY_DIM = 128
DN_VALUE_DIM = 128
CONV_WIDTH = 4
CHUNK = 64
D_FF = 4 * D_MODEL
LN_EPS = 1e-5
RMS_EPS = 1e-6
DEPTH = 1
DEEPNORM_ALPHA = (2 * DEPTH) ** 0.25

ATTN_Q_W = ATTN_HEADS * ATTN_HEAD_DIM
ATTN_KV_W = ATTN_KV_HEADS * ATTN_HEAD_DIM
DN_QK_W = DN_HEADS * DN_KEY_DIM
DN_V_W = DN_HEADS * DN_VALUE_DIM
DN_CONV_W = 2 * DN_QK_W + DN_V_W

LANES = 128
SUBLANES = 8
VMEM_LIMIT = 56 * 1024 * 1024

PROJ_TM = 512
PROJ_CONV_GROUP = 2 * DN_KEY_DIM
ATTN_TQ = 512
DN_PREP_STEP = 512
DN_STEP = 256
DN_MAT_PITCH = CHUNK + SUBLANES
FFN_TM = 512
FFN_CHUNK = 1024

BF16 = jnp.bfloat16
F32 = jnp.float32


def _dot(a, b):
    return jnp.dot(a, b, preferred_element_type=F32)


def _dot_nt(a, b):
    return lax.dot_general(a, b, (((1,), (1,)), ((), ())), preferred_element_type=F32)


def _dot_tn(a, b):
    return lax.dot_general(a, b, (((0,), (0,)), ((), ())), preferred_element_type=F32)


def _sigmoid(x):
    return 0.5 + 0.5 * jnp.tanh(0.5 * x)


def _const_spec(shape):
    return pl.BlockSpec(shape, lambda *_: (0,) * len(shape), pipeline_mode=pl.Buffered(1))


def _proj_kernel(x_ref, w_in_ref, conv_ref, gpar_ref,
                 aqkv_ref, dq_ref, dk_ref, dkt_ref, dv_ref, dzs_ref, gbc_ref, gbr_ref,
                 w_attn_ref, w_dqkv_ref, w_dz_ref, w_ba_ref, *pbuf_refs, tiles_per_seq):
    i = pl.program_id(0)
    tm = x_ref.shape[0]
    xb = x_ref[...].astype(BF16)
    first = i % tiles_per_seq == 0
    half_cw = 0.5 * conv_ref[...]
    grp = PROJ_CONV_GROUP

    @pl.when(i == 0)
    def _():
        for pbuf_ref in pbuf_refs:
            pbuf_ref[tm:tm + SUBLANES, :] = jnp.zeros((SUBLANES, grp), F32)
        lo_half = lax.broadcasted_iota(jnp.int32, (D_MODEL, LANES), 1) < ATTN_HEAD_DIM
        w_attn_ref[:, :ATTN_Q_W] = (w_in_ref[:, :ATTN_Q_W] * (ATTN_HEAD_DIM ** -0.5)).astype(BF16)
        for n, c0 in enumerate((ATTN_Q_W, ATTN_Q_W + ATTN_KV_W)):
            pair = w_in_ref[:, c0:c0 + ATTN_KV_W]
            swapped = pltpu.roll(pair, ATTN_HEAD_DIM, 1)
            base = ATTN_Q_W + 2 * n * ATTN_KV_W
            w_attn_ref[:, base:base + LANES] = jnp.where(lo_half, pair, swapped).astype(BF16)
            w_attn_ref[:, base + LANES:base + 2 * LANES] = jnp.where(lo_half, swapped, pair).astype(BF16)
        c_dqkv = ATTN_Q_W + 2 * ATTN_KV_W
        for c0 in range(0, DN_CONV_W, grp):
            w_dqkv_ref[:, c0:c0 + grp] = w_in_ref[:, c_dqkv + c0:c_dqkv + c0 + grp].astype(BF16)
        c_dz = c_dqkv + DN_CONV_W
        w_dz_ref[...] = w_in_ref[:, c_dz:c_dz + DN_V_W].astype(BF16)
        c_ba = c_dz + DN_V_W
        keep = lax.broadcasted_iota(jnp.int32, (D_MODEL, LANES), 1) < 2 * DN_HEADS
        w_ba_ref[...] = jnp.where(keep, w_in_ref[:, c_ba:c_ba + LANES], 0.0).astype(BF16)

    def silu_of_twice(h):
        return h + h * jnp.tanh(h)

    def conv_matmul(gi):
        pbuf_ref = pbuf_refs[gi]
        carry = pbuf_ref[tm:tm + SUBLANES, :]
        pbuf_ref[0:SUBLANES, :] = jnp.where(first, 0.0, carry)
        pbuf_ref[SUBLANES:SUBLANES + tm, :] = _dot(xb, w_dqkv_ref[:, gi * grp:(gi + 1) * grp])

    def conv_silu(gi):
        h = None
        for s in range(CONV_WIDTH):
            taps = half_cw[CONV_WIDTH - 1 - s:CONV_WIDTH - s, gi * grp:(gi + 1) * grp]
            term = pbuf_refs[gi][pl.ds(SUBLANES - s, tm), :] * taps
            h = term if h is None else h + term
        return silu_of_twice(h)

    def unit_rows(yh, scale):
        return yh * (lax.rsqrt(jnp.sum(yh * yh, axis=-1, keepdims=True) + RMS_EPS) * scale)

    def q_tail(c0, _):
        y = conv_silu(c0 // grp)
        for lo in range(0, grp, DN_KEY_DIM):
            dq_ref[:, c0 + lo:c0 + lo + DN_KEY_DIM] = unit_rows(y[:, lo:lo + DN_KEY_DIM], DN_KEY_DIM ** -0.5).astype(BF16)

    def k_tail(c0, _):
        y = conv_silu((DN_QK_W + c0) // grp)
        for lo in range(0, grp, DN_KEY_DIM):
            kh = unit_rows(y[:, lo:lo + DN_KEY_DIM], 1.0).astype(BF16)
            dk_ref[:, c0 + lo:c0 + lo + DN_KEY_DIM] = kh
            dkt_ref[c0 + lo:c0 + lo + DN_KEY_DIM, :] = kh.astype(F32).T.astype(BF16)

    def v_tail(c0, _):
        dv_ref[:, c0:c0 + grp] = conv_silu((2 * DN_QK_W + c0) // grp).astype(BF16)

    def attn_tail(c0, a):
        aqkv_ref[:, c0:c0 + ATTN_Q_W] = a.astype(BF16)

    def dz_tail(_, a):
        dzs_ref[...] = silu_of_twice(0.5 * a).astype(BF16)

    def gate_tail(_, ba):
        gpar = gpar_ref[...]
        lane = lax.broadcasted_iota(jnp.int32, (tm, LANES), 1)
        row = lax.broadcasted_iota(jnp.int32, (tm, LANES), 0)
        beta = _sigmoid(ba)
        z = ba + gpar[1:2, :]
        softplus = jnp.maximum(z, 0.0) + jnp.log1p(jnp.exp(-jnp.abs(z)))
        g = -jnp.exp(gpar[0:1, :]) * softplus
        s = 1
        while s < CHUNK:
            g = g + jnp.where(row % CHUNK >= s, pltpu.roll(g, s, 0), 0.0)
            s *= 2
        gb = jnp.where(lane < DN_HEADS, beta, jnp.where(lane < 2 * DN_HEADS, g, 0.0))
        gbc_ref[...] = gb
        gbr_ref[...] = gb.T[:SUBLANES, :]

    conv_stages = []
    for c0 in range(0, DN_QK_W, grp):
        conv_stages.append((functools.partial(conv_matmul, c0 // grp), q_tail, c0))
    for c0 in range(0, DN_QK_W, grp):
        conv_stages.append((functools.partial(conv_matmul, (DN_QK_W + c0) // grp), k_tail, c0))
    for c0 in range(0, DN_V_W, grp):
        conv_stages.append((functools.partial(conv_matmul, (2 * DN_QK_W + c0) // grp), v_tail, c0))
    plain_stages = []
    for c0 in range(0, w_attn_ref.shape[1], ATTN_Q_W):
        plain_stages.append((functools.partial(lambda c: _dot(xb, w_attn_ref[:, c:c + ATTN_Q_W]), c0), attn_tail, c0))
    plain_stages.append((lambda: _dot(xb, w_dz_ref[...]), dz_tail, 0))
    plain_stages.append((lambda: _dot(xb, w_ba_ref[...]), gate_tail, 0))
    stages = []
    for n, st in enumerate(conv_stages):
        stages.append(st)
        if n < len(plain_stages):
            stages.append(plain_stages[n])
    stages += plain_stages[len(conv_stages):]

    pending = stages[0][0]()
    for n, (_, tail, c0) in enumerate(stages):
        ready = pending
        if n + 1 < len(stages):
            pending = stages[n + 1][0]()
        tail(c0, ready)


def _project(x2, w_in, conv_w, gpar, batch, seq):
    n = x2.shape[0]
    tm = PROJ_TM
    tiles_per_seq = seq // tm
    attn_w = ATTN_Q_W + 4 * ATTN_KV_W
    row_spec = lambda width: pl.BlockSpec((tm, width), lambda i: (i, 0))
    seq_spec = lambda rows: pl.BlockSpec((None, rows, tm), lambda i: (i // tiles_per_seq, 0, i % tiles_per_seq))
    out_shape = (
        jax.ShapeDtypeStruct((n, attn_w), BF16),
        jax.ShapeDtypeStruct((n, DN_QK_W), BF16),
        jax.ShapeDtypeStruct((n, DN_QK_W), BF16),
        jax.ShapeDtypeStruct((batch, DN_QK_W, seq), BF16),
        jax.ShapeDtypeStruct((n, DN_V_W), BF16),
        jax.ShapeDtypeStruct((n, DN_V_W), BF16),
        jax.ShapeDtypeStruct((n, LANES), F32),
        jax.ShapeDtypeStruct((batch, SUBLANES, seq), F32),
    )
    return pl.pallas_call(
        functools.partial(_proj_kernel, tiles_per_seq=tiles_per_seq),
        grid=(n // tm,),
        in_specs=[
            row_spec(D_MODEL),
            _const_spec(w_in.shape), _const_spec(conv_w.shape), _const_spec(gpar.shape),
        ],
        out_specs=(
            row_spec(attn_w), row_spec(DN_QK_W), row_spec(DN_QK_W), seq_spec(DN_QK_W),
            row_spec(DN_V_W), row_spec(DN_V_W), row_spec(LANES), seq_spec(SUBLANES),
        ),
        out_shape=out_shape,
        scratch_shapes=[pltpu.VMEM((D_MODEL, attn_w), BF16), pltpu.VMEM((D_MODEL, DN_CONV_W), BF16),
                        pltpu.VMEM((D_MODEL, DN_V_W), BF16), pltpu.VMEM((D_MODEL, LANES), BF16)]
        + [pltpu.VMEM((SUBLANES + tm, PROJ_CONV_GROUP), F32)] * (DN_CONV_W // PROJ_CONV_GROUP),
        compiler_params=pltpu.CompilerParams(dimension_semantics=("arbitrary",), vmem_limit_bytes=VMEM_LIMIT),
        name="proj_dnprep",
    )(x2, w_in, conv_w, gpar)


def _attn_kernel(sink_ref, q_ref, kv_ref, kvp_ref, o_ref):
    j = pl.program_id(1)
    tq = q_ref.shape[0]
    w = WINDOW
    pair_w = 2 * ATTN_HEAD_DIM
    group = ATTN_HEADS // ATTN_KV_HEADS
    lo = lax.broadcasted_iota(jnp.int32, (w, pair_w), 1) < ATTN_HEAD_DIM
    qi = lax.broadcasted_iota(jnp.int32, (w, 2 * w), 0)
    kj = lax.broadcasted_iota(jnp.int32, (w, 2 * w), 1)
    band = (kj > qi) & (kj <= qi + w)
    zero = jnp.zeros((), BF16)

    def two_windows(wi, col):
        cols = slice(col * pair_w, (col + 1) * pair_w)
        prev = kvp_ref[:, cols] if wi == 0 else kv_ref[(wi - 1) * w:wi * w, cols]
        return jnp.concatenate([prev, kv_ref[wi * w:(wi + 1) * w, cols]], axis=0)

    def scores(wi, hk):
        parts = []
        for pp in range(group // 2):
            pair = hk * (group // 2) + pp
            qp = q_ref[wi * w:(wi + 1) * w, pair * pair_w:(pair + 1) * pair_w]
            parts += [jnp.where(lo, qp, zero), jnp.where(lo, zero, qp)]
        return _dot_nt(jnp.concatenate(parts, axis=0), two_windows(wi, hk))

    def finish(wi, hk, s):
        mask = band & ((kj >= w) | (j > 0)) if wi == 0 else band
        ps, invs = [], []
        for e in range(group):
            se = jnp.where(mask, s[e * w:(e + 1) * w], -1e30)
            sink = sink_ref[hk * group + e]
            m = jnp.maximum(jnp.max(se, axis=-1, keepdims=True), sink)
            pe = jnp.exp(se - m)
            den = jnp.sum(pe, axis=-1, keepdims=True) + jnp.exp(sink - m)
            ps.append(pe.astype(BF16))
            invs.append(1.0 / den)
        o = _dot(jnp.concatenate(ps, axis=0), two_windows(wi, ATTN_KV_HEADS + hk))
        for pp in range(group // 2):
            pair = hk * (group // 2) + pp
            oa = o[(2 * pp) * w:(2 * pp + 1) * w] * invs[2 * pp]
            ob = o[(2 * pp + 1) * w:(2 * pp + 2) * w] * invs[2 * pp + 1]
            o_ref[wi * w:(wi + 1) * w, pair * pair_w:(pair + 1) * pair_w] = jnp.where(lo, oa, ob).astype(BF16)

    units = [(wi, hk) for wi in range(tq // w) for hk in range(ATTN_KV_HEADS)]
    s_next = scores(*units[0])
    for n, unit in enumerate(units):
        s = s_next
        if n + 1 < len(units):
            s_next = scores(*units[n + 1])
        finish(*unit, s)


def _attention(aqkv, sinks, batch, seq):
    n = aqkv.shape[0]
    tq = ATTN_TQ
    nq = seq // tq
    wpt = tq // WINDOW
    kv_w = aqkv.shape[1] - ATTN_Q_W
    assert kv_w == ATTN_Q_W
    return pl.pallas_call(
        _attn_kernel,
        grid=(batch, nq),
        in_specs=[
            pl.BlockSpec(memory_space=pltpu.SMEM),
            pl.BlockSpec((tq, ATTN_Q_W), lambda b, j: (b * nq + j, 0)),
            pl.BlockSpec((tq, kv_w), lambda b, j: (b * nq + j, 1)),
            pl.BlockSpec((WINDOW, kv_w), lambda b, j: (jnp.maximum((b * nq + j) * wpt - 1, 0), 1)),
        ],
        out_specs=pl.BlockSpec((tq, ATTN_Q_W), lambda b, j: (b * nq + j, 0)),
        out_shape=jax.ShapeDtypeStruct((n, ATTN_Q_W), BF16),
        compiler_params=pltpu.CompilerParams(dimension_semantics=("arbitrary", "arbitrary"),
                                             vmem_limit_bytes=VMEM_LIMIT),
        name="swa_sinks",
    )(sinks, aqkv, aqkv, aqkv)


def _chunk_gates(gbc_ref, gbr_ref, b, h, r0):
    c = CHUNK
    beta_c = gbc_ref[b, r0:r0 + c, h:h + 1]
    g_c = gbc_ref[b, r0:r0 + c, DN_HEADS + h:DN_HEADS + h + 1]
    beta_r = gbr_ref[b, h:h + 1, r0:r0 + c]
    g_r = gbr_ref[b, DN_HEADS + h:DN_HEADS + h + 1, r0:r0 + c]
    return beta_c, g_c, beta_r, g_r


def _delta_prep_kernel(q_ref, k_ref, v_ref, gbc_ref, gbr_ref, u_ref, w_ref, qg_ref, attn_ref,
                       mat_ref, at_ref, x_ref):
    batch = q_ref.shape[0]
    c = CHUNK
    n_chunks = q_ref.shape[1] // c
    items = [(cc, b, h) for cc in range(n_chunks) for b in range(batch) for h in range(DN_HEADS)]
    assert len(items) == LANES and c * 2 == LANES
    pitch = DN_MAT_PITCH

    row = lax.broadcasted_iota(jnp.int32, (c, LANES), 0)
    col = lax.broadcasted_iota(jnp.int32, (c, LANES), 1)
    causal = row >= col
    strict = row > col

    for m, (cc, b, h) in enumerate(items):
        r0, lo = cc * c, h * DN_KEY_DIM
        k = k_ref[b, r0:r0 + c, lo:lo + DN_KEY_DIM]
        q = q_ref[b, r0:r0 + c, lo:lo + DN_KEY_DIM]
        s = _dot_nt(jnp.concatenate([q, k], axis=0), jnp.concatenate([k, k], axis=0))
        beta_c, g_c, _, g_r = _chunk_gates(gbc_ref, gbr_ref, b, h, r0)
        g_cb = jnp.broadcast_to(g_c, (c, LANES))
        qg_ref[b, r0:r0 + c, lo:lo + DN_KEY_DIM] = (q.astype(F32) * jnp.exp(g_cb)).astype(BF16)
        g_r2 = jnp.concatenate([g_r, g_r], axis=1)
        decay = jnp.exp(jnp.where(causal, g_cb - g_r2, -1e30))
        attn_ref[b, h, r0:r0 + c, :] = (s[:c] * decay)[:, :c].astype(BF16)
        mat_ref[m * pitch:m * pitch + c, :] = jnp.where(strict, beta_c * s[c:] * decay, 0.0)

    for i in range(1, c):
        at_ref[i] = mat_ref[pl.ds(i, LANES, stride=pitch), :].T[:c, :]

    sub = lax.broadcasted_iota(jnp.int32, (SUBLANES, LANES), 0)
    zero_blk = jnp.zeros((SUBLANES, LANES), F32)
    for i in range(c):
        nblk = i // SUBLANES + 1
        acc = [zero_blk] * (nblk - 1) + [(sub == i % SUBLANES).astype(F32)]
        for j in range(i):
            a_ij = jnp.broadcast_to(at_ref[i, j:j + 1, :], (SUBLANES, LANES))
            for blk in range(j // SUBLANES + 1):
                acc[blk] = acc[blk] - a_ij * x_ref[j, blk * SUBLANES:(blk + 1) * SUBLANES, :]
        for blk in range(c // SUBLANES):
            x_ref[i, blk * SUBLANES:(blk + 1) * SUBLANES, :] = acc[blk] if blk < nblk else zero_blk

    zeros_half = jnp.zeros((LANES - c, LANES), F32)
    for i in range(c):
        mat_ref[pl.ds(i, LANES, stride=pitch), :] = jnp.concatenate([x_ref[i], zeros_half], axis=0).T

    for m, (cc, b, h) in enumerate(items):
        r0, lo = cc * c, h * DN_KEY_DIM
        _, _, beta_r, g_r = _chunk_gates(gbc_ref, gbr_ref, b, h, r0)
        t = mat_ref[m * pitch:m * pitch + c, :][:, :c]
        u_ref[b, r0:r0 + c, lo:lo + DN_VALUE_DIM] = _dot((t * beta_r).astype(BF16), v_ref[b, r0:r0 + c, lo:lo + DN_VALUE_DIM])
        w = _dot((t * (beta_r * jnp.exp(g_r))).astype(BF16), k_ref[b, r0:r0 + c, lo:lo + DN_KEY_DIM])
        w_ref[b, r0:r0 + c, lo:lo + DN_KEY_DIM] = w.astype(BF16)


def _delta_prep(dq, dk, dv, gbc, gbr, batch, seq):
    ts = DN_PREP_STEP
    tok_spec = lambda width: pl.BlockSpec((batch, ts, width), lambda i: (0, i, 0))
    return pl.pallas_call(
        _delta_prep_kernel,
        grid=(seq // ts,),
        in_specs=[
            tok_spec(DN_QK_W), tok_spec(DN_QK_W), tok_spec(DN_V_W), tok_spec(LANES),
            pl.BlockSpec((batch, SUBLANES, ts), lambda i: (0, 0, i)),
        ],
        out_specs=(
            tok_spec(DN_V_W), tok_spec(DN_QK_W), tok_spec(DN_QK_W),
            pl.BlockSpec((batch, DN_HEADS, ts, CHUNK), lambda i: (0, 0, i, 0)),
        ),
        out_shape=(
            jax.ShapeDtypeStruct((batch, seq, DN_V_W), F32),
            jax.ShapeDtypeStruct((batch, seq, DN_QK_W), BF16),
            jax.ShapeDtypeStruct((batch, seq, DN_QK_W), BF16),
            jax.ShapeDtypeStruct((batch, DN_HEADS, seq, CHUNK), BF16),
        ),
        scratch_shapes=[
            pltpu.VMEM((LANES * DN_MAT_PITCH, LANES), F32),
            pltpu.VMEM((CHUNK, CHUNK, LANES), F32),
            pltpu.VMEM((CHUNK, CHUNK, LANES), F32),
        ],
        compiler_params=pltpu.CompilerParams(dimension_semantics=("arbitrary",), vmem_limit_bytes=VMEM_LIMIT),
        name="delta_prep",
    )(dq, dk, dv, gbc, gbr)


def _delta_scan_kernel(qg_ref, kt_ref, u_ref, w_ref, attn_ref, z_ref, gbr_ref, nw_ref, o_ref, state_ref):
    step = pl.program_id(0)
    batch = qg_ref.shape[0]
    c = CHUNK
    ts = qg_ref.shape[1]
    per_tile = LANES // c
    assert ts % LANES == 0

    @pl.when(step == 0)
    def _():
        state_ref[...] = jnp.zeros_like(state_ref)

    nw = nw_ref[...]
    chains = [(b, h) for b in range(batch) for h in range(DN_HEADS)]
    lane = lax.broadcasted_iota(jnp.int32, (1, LANES), 1)

    k_dec_t, s_decay = {}, {}
    for (b, h) in chains:
        for t0 in range(0, ts, LANES):
            kt = kt_ref[b, h * DN_KEY_DIM:(h + 1) * DN_KEY_DIM, t0:t0 + LANES].astype(F32)
            g_r = gbr_ref[b, DN_HEADS + h:DN_HEADS + h + 1, t0:t0 + LANES]
            for sub in range(per_tile):
                cc = t0 // c + sub
                g_last = g_r[:, (sub + 1) * c - 1:(sub + 1) * c]
                in_chunk = (lane >= sub * c) & (lane < (sub + 1) * c)
                k_dec_t[(cc, b, h)] = (kt * jnp.exp(jnp.where(in_chunk, g_last - g_r, -1e30))).astype(BF16)
                s_decay[(cc, b, h)] = jnp.exp(g_last)

    for cc in range(ts // c):
        r0 = cc * c
        res, vb, om, kv = {}, {}, {}, {}
        for (b, h) in chains:
            lo = h * DN_KEY_DIM
            lhs = jnp.concatenate([w_ref[b, r0:r0 + c, lo:lo + DN_KEY_DIM],
                                   qg_ref[b, r0:r0 + c, lo:lo + DN_KEY_DIM]], axis=0)
            res[(b, h)] = _dot(lhs, state_ref[b * DN_HEADS + h].astype(BF16))
        for (b, h) in chains:
            lo = h * DN_VALUE_DIM
            vb[(b, h)] = (u_ref[b, r0:r0 + c, lo:lo + DN_VALUE_DIM] - res[(b, h)][:c]).astype(BF16)
        for (b, h) in chains:
            om[(b, h)] = _dot(attn_ref[b, h, r0:r0 + c, :], vb[(b, h)])
            kv[(b, h)] = _dot(k_dec_t[(cc, b, h)], jnp.concatenate([vb[(b, h)]] * per_tile, axis=0))
        for (b, h) in chains:
            idx = b * DN_HEADS + h
            state_ref[idx] = state_ref[idx] * s_decay[(cc, b, h)] + kv[(b, h)]
        for (b, h) in chains:
            lo = h * DN_VALUE_DIM
            o = res[(b, h)][c:] + om[(b, h)]
            o = o * lax.rsqrt(jnp.mean(o * o, axis=-1, keepdims=True) + RMS_EPS) * nw
            o = o * z_ref[b, r0:r0 + c, lo:lo + DN_VALUE_DIM].astype(F32)
            o_ref[b, r0:r0 + c, lo:lo + DN_VALUE_DIM] = o.astype(BF16)


def _delta_scan(qg, dkt, u, w, attn, dzs, gbr, norm_w, batch, seq):
    ts = DN_STEP
    tok_spec = lambda width: pl.BlockSpec((batch, ts, width), lambda i: (0, i, 0))
    seq_spec = lambda rows: pl.BlockSpec((batch, rows, ts), lambda i: (0, 0, i))
    return pl.pallas_call(
        _delta_scan_kernel,
        grid=(seq // ts,),
        in_specs=[
            tok_spec(DN_QK_W), seq_spec(DN_QK_W), tok_spec(DN_V_W), tok_spec(DN_QK_W),
            pl.BlockSpec((batch, DN_HEADS, ts, CHUNK), lambda i: (0, 0, i, 0)),
            tok_spec(DN_V_W), seq_spec(SUBLANES),
            _const_spec(norm_w.shape),
        ],
        out_specs=tok_spec(DN_V_W),
        out_shape=jax.ShapeDtypeStruct((batch, seq, DN_V_W), BF16),
        scratch_shapes=[pltpu.VMEM((batch * DN_HEADS, DN_KEY_DIM, DN_VALUE_DIM), F32)],
        compiler_params=pltpu.CompilerParams(dimension_semantics=("arbitrary",), vmem_limit_bytes=VMEM_LIMIT),
        name="delta_scan",
    )(qg, dkt, u, w, attn, dzs, gbr, norm_w)


def _layer_norm(x, g, b):
    mu = jnp.mean(x, axis=-1, keepdims=True)
    xc = x - mu
    var = jnp.mean(xc * xc, axis=-1, keepdims=True)
    return xc * lax.rsqrt(var + LN_EPS) * g + b


def _merge_ffn_kernel(x_ref, oa_ref, ob_ref, wg_ref, wao_ref, wdo_ref, wout_ref, ln1_ref,
                      wup_ref, wdown_ref, ln2_ref, out_ref):
    x = x_ref[...]
    xb = x.astype(BF16)
    merged = _sigmoid(_dot(xb, wg_ref[:, :D_MODEL])) * _dot(oa_ref[...], wao_ref[...])
    merged = merged + _sigmoid(_dot(xb, wg_ref[:, D_MODEL:])) * _dot(ob_ref[...], wdo_ref[...])
    mix = _dot(merged.astype(BF16), wout_ref[...])
    ln1 = ln1_ref[...]
    x1 = _layer_norm(DEEPNORM_ALPHA * x + mix, ln1[0:1, :], ln1[1:2, :])
    x1b = x1.astype(BF16)
    ffn = None
    for c0 in range(0, D_FF, FFN_CHUNK):
        h = jnp.maximum(_dot(x1b, wup_ref[:, c0:c0 + FFN_CHUNK]), 0.0)
        part = _dot((h * h).astype(BF16), wdown_ref[c0:c0 + FFN_CHUNK, :])
        ffn = part if ffn is None else ffn + part
    ln2 = ln2_ref[...]
    out_ref[...] = _layer_norm(DEEPNORM_ALPHA * x1 + ffn, ln2[0:1, :], ln2[1:2, :])


def _merge_ffn(x2, oa, ob, wg, wao, wdo, wout, ln1, wup, wdown, ln2):
    n = x2.shape[0]
    tm = FFN_TM
    row_spec = lambda width: pl.BlockSpec((tm, width), lambda i: (i, 0))
    return pl.pallas_call(
        _merge_ffn_kernel,
        grid=(n // tm,),
        in_specs=[
            row_spec(D_MODEL), row_spec(ATTN_Q_W), row_spec(DN_V_W),
            _const_spec(wg.shape), _const_spec(wao.shape), _const_spec(wdo.shape), _const_spec(wout.shape),
            _const_spec(ln1.shape), _const_spec(wup.shape), _const_spec(wdown.shape), _const_spec(ln2.shape),
        ],
        out_specs=row_spec(D_MODEL),
        out_shape=jax.ShapeDtypeStruct((n, D_MODEL), F32),
        compiler_params=pltpu.CompilerParams(dimension_semantics=("arbitrary",), vmem_limit_bytes=VMEM_LIMIT),
        name="merge_ffn",
    )(x2, oa, ob, wg, wao, wdo, wout, ln1, wup, wdown, ln2)


def _layer(x, w_in, conv_w, attn_sinks, dn_a_log, dn_dt_bias, dn_norm_w, w_attn_out, w_dn_out, w_out,
           ln1_g, ln1_b, w_up, w_down, ln2_g, ln2_b):
    batch, seq, _ = x.shape
    n = batch * seq
    x2 = x.reshape(n, D_MODEL)

    c_gates = ATTN_Q_W + 2 * ATTN_KV_W + DN_CONV_W + DN_V_W + 2 * DN_HEADS
    w_gates = w_in[:, c_gates:].astype(BF16)
    gpar = jnp.zeros((2, LANES), F32)
    gpar = gpar.at[0, DN_HEADS:2 * DN_HEADS].set(dn_a_log.astype(F32))
    gpar = gpar.at[1, DN_HEADS:2 * DN_HEADS].set(dn_dt_bias.astype(F32))

    aqkv, dq, dk, dkt, dv, dzs, gbc, gbr = _project(
        x2, w_in.astype(F32), conv_w.astype(F32), gpar, batch, seq)
    oa = _attention(aqkv, attn_sinks.astype(F32), batch, seq)
    r3 = lambda t: t.reshape(batch, seq, t.shape[-1])
    u, w, qg, attn = _delta_prep(r3(dq), r3(dk), r3(dv), r3(gbc), gbr, batch, seq)
    ob = _delta_scan(qg, dkt, u, w, attn, r3(dzs), gbr,
                     dn_norm_w.astype(F32).reshape(1, DN_VALUE_DIM), batch, seq)
    out = _merge_ffn(
        x2, oa, ob.reshape(n, DN_V_W), w_gates, w_attn_out.astype(BF16), w_dn_out.astype(BF16),
        w_out.astype(BF16), jnp.stack([ln1_g, ln1_b]).astype(F32), w_up.astype(BF16), w_down.astype(BF16),
        jnp.stack([ln2_g, ln2_b]).astype(F32))
    return out.reshape(batch, seq, D_MODEL)


def kernel(x, w_in, conv_w, attn_sinks, dn_a_log, dn_dt_bias, dn_norm_w, w_attn_out, w_dn_out, w_out,
           ln1_g, ln1_b, w_up, w_down, ln2_g, ln2_b):
    for l in range(DEPTH):
        x = _layer(x, w_in[l], conv_w[l], attn_sinks[l], dn_a_log[l], dn_dt_bias[l], dn_norm_w[l],
                   w_attn_out[l], w_dn_out[l], w_out[l], ln1_g[l], ln1_b[l], w_up[l], w_down[l],
                   ln2_g[l], ln2_b[l])
    return x
```

```python
import functools

import jax
import jax.numpy as jnp
from jax import lax
from jax.experimental import pallas as pl
from jax.experimental.pallas import tpu as pltpu

D_MODEL = 1024
ATTN_HEADS = 8
ATTN_KV_HEADS = 2
ATTN_HEAD_DIM = 64
WINDOW = 128
DN_HEADS = 4
DN_KEY_DIM = 128
DN_VALUE_DIM = 128
CONV_WIDTH = 4
CHUNK = 64
D_FF = 4 * D_MODEL
LN_EPS = 1e-5
RMS_EPS = 1e-6
DEPTH = 1
DEEPNORM_ALPHA = (2 * DEPTH) ** 0.25

ATTN_Q_W = ATTN_HEADS * ATTN_HEAD_DIM
ATTN_KV_W = ATTN_KV_HEADS * ATTN_HEAD_DIM
DN_QK_W = DN_HEADS * DN_KEY_DIM
DN_V_W = DN_HEADS * DN_VALUE_DIM
DN_CONV_W = 2 * DN_QK_W + DN_V_W

LANES = 128
SUBLANES = 8
VMEM_LIMIT = 56 * 1024 * 1024

PROJ_TM = 512
PROJ_CONV_GROUP = 2 * DN_KEY_DIM
ATTN_TQ = 512
DN_PREP_STEP = 512
DN_STEP = 256
DN_MAT_PITCH = CHUNK + SUBLANES
FFN_TM = 512
FFN_CHUNK = 1024

BF16 = jnp.bfloat16
F32 = jnp.float32


def _dot(a, b):
    return jnp.dot(a, b, preferred_element_type=F32)


def _dot_nt(a, b):
    return lax.dot_general(a, b, (((1,), (1,)), ((), ())), preferred_element_type=F32)


def _dot_tn(a, b):
    return lax.dot_general(a, b, (((0,), (0,)), ((), ())), preferred_element_type=F32)


def _sigmoid(x):
    return 0.5 + 0.5 * jnp.tanh(0.5 * x)


def _const_spec(shape):
    return pl.BlockSpec(shape, lambda *_: (0,) * len(shape), pipeline_mode=pl.Buffered(1))


def _proj_kernel(x_ref, w_t_ref, conv_ref, gpar_ref,
                 aqkv_ref, dq_ref, dk_ref, dkt_ref, dv_ref, dzs_ref, gbc_ref, gbr_ref,
                 w_attn_ref, w_dqkv_ref, w_dz_ref, w_ba_ref, *pbuf_refs, tiles_per_seq):
    i = pl.program_id(0)
    tm = x_ref.shape[0]
    xb = x_ref[...].astype(BF16)
    first = i % tiles_per_seq == 0
    half_cw = 0.5 * conv_ref[...]
    grp = PROJ_CONV_GROUP

    @pl.when(i == 0)
    def _():
        for pbuf_ref in pbuf_refs:
            pbuf_ref[tm:tm + SUBLANES, :] = jnp.zeros((SUBLANES, grp), F32)
        def w_cols(c0, width=LANES):
            return w_t_ref[c0:c0 + width, :].T

        lo_half = lax.broadcasted_iota(jnp.int32, (D_MODEL, LANES), 1) < ATTN_HEAD_DIM
        for c0 in range(0, ATTN_Q_W, LANES):
            w_attn_ref[:, c0:c0 + LANES] = (w_cols(c0) * (ATTN_HEAD_DIM ** -0.5)).astype(BF16)
        for n, c0 in enumerate((ATTN_Q_W, ATTN_Q_W + ATTN_KV_W)):
            pair = w_cols(c0, ATTN_KV_W)
            swapped = pltpu.roll(pair, ATTN_HEAD_DIM, 1)
            base = ATTN_Q_W + 2 * n * ATTN_KV_W
            w_attn_ref[:, base:base + LANES] = jnp.where(lo_half, pair, swapped).astype(BF16)
            w_attn_ref[:, base + LANES:base + 2 * LANES] = jnp.where(lo_half, swapped, pair).astype(BF16)
        c_dqkv = ATTN_Q_W + 2 * ATTN_KV_W
        for c0 in range(0, DN_CONV_W, LANES):
            w_dqkv_ref[:, c0:c0 + LANES] = w_cols(c_dqkv + c0).astype(BF16)
        c_dz = c_dqkv + DN_CONV_W
        for c0 in range(0, DN_V_W, LANES):
            w_dz_ref[:, c0:c0 + LANES] = w_cols(c_dz + c0).astype(BF16)
        c_ba = c_dz + DN_V_W
        keep = lax.broadcasted_iota(jnp.int32, (D_MODEL, LANES), 1) < 2 * DN_HEADS
        w_ba_ref[...] = jnp.where(keep, w_cols(c_ba), 0.0).astype(BF16)

    def silu_of_twice(h):
        return h + h * jnp.tanh(h)

    def conv_matmul(gi):
        pbuf_ref = pbuf_refs[gi]
        carry = pbuf_ref[tm:tm + SUBLANES, :]
        pbuf_ref[0:SUBLANES, :] = jnp.where(first, 0.0, carry)
        pbuf_ref[SUBLANES:SUBLANES + tm, :] = _dot(xb, w_dqkv_ref[:, gi * grp:(gi + 1) * grp])

    def conv_silu(gi):
        h = None
        for s in range(CONV_WIDTH):
            taps = half_cw[CONV_WIDTH - 1 - s:CONV_WIDTH - s, gi * grp:(gi + 1) * grp]
            term = pbuf_refs[gi][pl.ds(SUBLANES - s, tm), :] * taps
            h = term if h is None else h + term
        return silu_of_twice(h)

    def unit_rows(yh, scale):
        return yh * (lax.rsqrt(jnp.sum(yh * yh, axis=-1, keepdims=True) + RMS_EPS) * scale)

    def q_tail(c0, _):
        y = conv_silu(c0 // grp)
        for lo in range(0, grp, DN_KEY_DIM):
            dq_ref[:, c0 + lo:c0 + lo + DN_KEY_DIM] = unit_rows(y[:, lo:lo + DN_KEY_DIM], DN_KEY_DIM ** -0.5).astype(BF16)

    def k_tail(c0, _):
        y = conv_silu((DN_QK_W + c0) // grp)
        for lo in range(0, grp, DN_KEY_DIM):
            kh = unit_rows(y[:, lo:lo + DN_KEY_DIM], 1.0).astype(BF16)
            dk_ref[:, c0 + lo:c0 + lo + DN_KEY_DIM] = kh
            dkt_ref[c0 + lo:c0 + lo + DN_KEY_DIM, :] = kh.astype(F32).T.astype(BF16)

    def v_tail(c0, _):
        dv_ref[:, c0:c0 + grp] = conv_silu((2 * DN_QK_W + c0) // grp).astype(BF16)

    def attn_tail(c0, a):
        aqkv_ref[:, c0:c0 + ATTN_Q_W] = a.astype(BF16)

    def dz_tail(_, a):
        dzs_ref[...] = silu_of_twice(0.5 * a).astype(BF16)

    def gate_tail(_, ba):
        gpar = gpar_ref[...]
        lane = lax.broadcasted_iota(jnp.int32, (tm, LANES), 1)
        row = lax.broadcasted_iota(jnp.int32, (tm, LANES), 0)
        beta = _sigmoid(ba)
        z = ba + gpar[1:2, :]
        softplus = jnp.maximum(z, 0.0) + jnp.log1p(jnp.exp(-jnp.abs(z)))
        g = -jnp.exp(gpar[0:1, :]) * softplus
        s = 1
        while s < CHUNK:
            g = g + jnp.where(row % CHUNK >= s, pltpu.roll(g, s, 0), 0.0)
            s *= 2
        gb = jnp.where(lane < DN_HEADS, beta, jnp.where(lane < 2 * DN_HEADS, g, 0.0))
        gbc_ref[...] = gb
        gbr_ref[...] = gb.T[:SUBLANES, :]

    conv_stages = []
    for c0 in range(0, DN_QK_W, grp):
        conv_stages.append((functools.partial(conv_matmul, c0 // grp), q_tail, c0))
    for c0 in range(0, DN_QK_W, grp):
        conv_stages.append((functools.partial(conv_matmul, (DN_QK_W + c0) // grp), k_tail, c0))
    for c0 in range(0, DN_V_W, grp):
        conv_stages.append((functools.partial(conv_matmul, (2 * DN_QK_W + c0) // grp), v_tail, c0))
    plain_stages = []
    for c0 in range(0, w_attn_ref.shape[1], ATTN_Q_W):
        plain_stages.append((functools.partial(lambda c: _dot(xb, w_attn_ref[:, c:c + ATTN_Q_W]), c0), attn_tail, c0))
    plain_stages.append((lambda: _dot(xb, w_dz_ref[...]), dz_tail, 0))
    plain_stages.append((lambda: _dot(xb, w_ba_ref[...]), gate_tail, 0))
    stages = []
    for n, st in enumerate(conv_stages):
        stages.append(st)
        if n < len(plain_stages):
            stages.append(plain_stages[n])
    stages += plain_stages[len(conv_stages):]

    pending = stages[0][0]()
    for n, (_, tail, c0) in enumerate(stages):
        ready = pending
        if n + 1 < len(stages):
            pending = stages[n + 1][0]()
        tail(c0, ready)


def _project(x2, w_in, conv_w, gpar, batch, seq):
    n = x2.shape[0]
    tm = PROJ_TM
    tiles_per_seq = seq // tm
    attn_w = ATTN_Q_W + 4 * ATTN_KV_W
    row_spec = lambda width: pl.BlockSpec((tm, width), lambda i: (i, 0))
    seq_spec = lambda rows: pl.BlockSpec((None, rows, tm), lambda i: (i // tiles_per_seq, 0, i % tiles_per_seq))
    out_shape = (
        jax.ShapeDtypeStruct((n, attn_w), BF16),
        jax.ShapeDtypeStruct((n, DN_QK_W), BF16),
        jax.ShapeDtypeStruct((n, DN_QK_W), BF16),
        jax.ShapeDtypeStruct((batch, DN_QK_W, seq), BF16),
        jax.ShapeDtypeStruct((n, DN_V_W), BF16),
        jax.ShapeDtypeStruct((n, DN_V_W), BF16),
        jax.ShapeDtypeStruct((n, LANES), F32),
        jax.ShapeDtypeStruct((batch, SUBLANES, seq), F32),
    )
    return pl.pallas_call(
        functools.partial(_proj_kernel, tiles_per_seq=tiles_per_seq),
        grid=(n // tm,),
        in_specs=[
            row_spec(D_MODEL),
            _const_spec(w_in.shape), _const_spec(conv_w.shape), _const_spec(gpar.shape),
        ],
        out_specs=(
            row_spec(attn_w), row_spec(DN_QK_W), row_spec(DN_QK_W), seq_spec(DN_QK_W),
            row_spec(DN_V_W), row_spec(DN_V_W), row_spec(LANES), seq_spec(SUBLANES),
        ),
        out_shape=out_shape,
        scratch_shapes=[pltpu.VMEM((D_MODEL, attn_w), BF16), pltpu.VMEM((D_MODEL, DN_CONV_W), BF16),
                        pltpu.VMEM((D_MODEL, DN_V_W), BF16), pltpu.VMEM((D_MODEL, LANES), BF16)]
        + [pltpu.VMEM((SUBLANES + tm, PROJ_CONV_GROUP), F32)] * (DN_CONV_W // PROJ_CONV_GROUP),
        compiler_params=pltpu.CompilerParams(dimension_semantics=("arbitrary",), vmem_limit_bytes=VMEM_LIMIT),
        name="proj_dnprep",
    )(x2, w_in, conv_w, gpar)


def _attn_kernel(sink_ref, q_ref, kv_ref, kvp_ref, o_ref):
    j = pl.program_id(1)
    tq = q_ref.shape[0]
    w = WINDOW
    pair_w = 2 * ATTN_HEAD_DIM
    group = ATTN_HEADS // ATTN_KV_HEADS
    lo = lax.broadcasted_iota(jnp.int32, (w, pair_w), 1) < ATTN_HEAD_DIM
    qi = lax.broadcasted_iota(jnp.int32, (w, 2 * w), 0)
    kj = lax.broadcasted_iota(jnp.int32, (w, 2 * w), 1)
    band = (kj > qi) & (kj <= qi + w)
    zero = jnp.zeros((), BF16)

    def two_windows(wi, col):
        cols = slice(col * pair_w, (col + 1) * pair_w)
        prev = kvp_ref[:, cols] if wi == 0 else kv_ref[(wi - 1) * w:wi * w, cols]
        return jnp.concatenate([prev, kv_ref[wi * w:(wi + 1) * w, cols]], axis=0)

    def scores(wi, hk):
        parts = []
        for pp in range(group // 2):
            pair = hk * (group // 2) + pp
            qp = q_ref[wi * w:(wi + 1) * w, pair * pair_w:(pair + 1) * pair_w]
            parts += [jnp.where(lo, qp, zero), jnp.where(lo, zero, qp)]
        return _dot_nt(jnp.concatenate(parts, axis=0), two_windows(wi, hk))

    def finish(wi, hk, s):
        mask = band & ((kj >= w) | (j > 0)) if wi == 0 else band
        ps, invs = [], []
        for e in range(group):
            se = jnp.where(mask, s[e * w:(e + 1) * w], -1e30)
            sink = sink_ref[hk * group + e]
            m = jnp.maximum(jnp.max(se, axis=-1, keepdims=True), sink)
            pe = jnp.exp(se - m)
            den = jnp.sum(pe, axis=-1, keepdims=True) + jnp.exp(sink - m)
            ps.append(pe.astype(BF16))
            invs.append(1.0 / den)
        o = _dot(jnp.concatenate(ps, axis=0), two_windows(wi, ATTN_KV_HEADS + hk))
        for pp in range(group // 2):
            pair = hk * (group // 2) + pp
            oa = o[(2 * pp) * w:(2 * pp + 1) * w] * invs[2 * pp]
            ob = o[(2 * pp + 1) * w:(2 * pp + 2) * w] * invs[2 * pp + 1]
            o_ref[wi * w:(wi + 1) * w, pair * pair_w:(pair + 1) * pair_w] = jnp.where(lo, oa, ob).astype(BF16)

    units = [(wi, hk) for wi in range(tq // w) for hk in range(ATTN_KV_HEADS)]
    s_next = scores(*units[0])
    for n, unit in enumerate(units):
        s = s_next
        if n + 1 < len(units):
            s_next = scores(*units[n + 1])
        finish(*unit, s)


def _attention(aqkv, sinks, batch, seq):
    n = aqkv.shape[0]
    tq = ATTN_TQ
    nq = seq // tq
    wpt = tq // WINDOW
    kv_w = aqkv.shape[1] - ATTN_Q_W
    assert kv_w == ATTN_Q_W
    return pl.pallas_call(
        _attn_kernel,
        grid=(batch, nq),
        in_specs=[
            pl.BlockSpec(memory_space=pltpu.SMEM),
            pl.BlockSpec((tq, ATTN_Q_W), lambda b, j: (b * nq + j, 0)),
            pl.BlockSpec((tq, kv_w), lambda b, j: (b * nq + j, 1)),
            pl.BlockSpec((WINDOW, kv_w), lambda b, j: (jnp.maximum((b * nq + j) * wpt - 1, 0), 1)),
        ],
        out_specs=pl.BlockSpec((tq, ATTN_Q_W), lambda b, j: (b * nq + j, 0)),
        out_shape=jax.ShapeDtypeStruct((n, ATTN_Q_W), BF16),
        compiler_params=pltpu.CompilerParams(dimension_semantics=("arbitrary", "arbitrary"),
                                             vmem_limit_bytes=VMEM_LIMIT),
        name="swa_sinks",
    )(sinks, aqkv, aqkv, aqkv)


def _chunk_gates(gbc_ref, gbr_ref, b, h, r0):
    c = CHUNK
    beta_c = gbc_ref[b, r0:r0 + c, h:h + 1]
    g_c = gbc_ref[b, r0:r0 + c, DN_HEADS + h:DN_HEADS + h + 1]
    beta_r = gbr_ref[b, h:h + 1, r0:r0 + c]
    g_r = gbr_ref[b, DN_HEADS + h:DN_HEADS + h + 1, r0:r0 + c]
    return beta_c, g_c, beta_r, g_r


def _delta_prep_kernel(q_ref, k_ref, v_ref, gbc_ref, gbr_ref, u_ref, w_ref, qg_ref, attn_ref,
                       mat_ref, at_ref, x_ref):
    batch = q_ref.shape[0]
    c = CHUNK
    n_chunks = q_ref.shape[1] // c
    items = [(cc, b, h) for cc in range(n_chunks) for b in range(batch) for h in range(DN_HEADS)]
    assert len(items) == LANES and c * 2 == LANES
    pitch = DN_MAT_PITCH

    row = lax.broadcasted_iota(jnp.int32, (c, LANES), 0)
    col = lax.broadcasted_iota(jnp.int32, (c, LANES), 1)
    causal = row >= col
    strict = row > col

    for m, (cc, b, h) in enumerate(items):
        r0, lo = cc * c, h * DN_KEY_DIM
        k = k_ref[b, r0:r0 + c, lo:lo + DN_KEY_DIM]
        q = q_ref[b, r0:r0 + c, lo:lo + DN_KEY_DIM]
        s = _dot_nt(jnp.concatenate([q, k], axis=0), jnp.concatenate([k, k], axis=0))
        beta_c, g_c, _, g_r = _chunk_gates(gbc_ref, gbr_ref, b, h, r0)
        g_cb = jnp.broadcast_to(g_c, (c, LANES))
        qg_ref[b, r0:r0 + c, lo:lo + DN_KEY_DIM] = (q.astype(F32) * jnp.exp(g_cb)).astype(BF16)
        g_r2 = jnp.concatenate([g_r, g_r], axis=1)
        decay = jnp.exp(jnp.where(causal, g_cb - g_r2, -1e30))
        attn_ref[b, h, r0:r0 + c, :] = (s[:c] * decay)[:, :c].astype(BF16)
        mat_ref[m * pitch:m * pitch + c, :] = jnp.where(strict, beta_c * s[c:] * decay, 0.0)

    for i in range(1, c):
        at_ref[i] = mat_ref[pl.ds(i, LANES, stride=pitch), :].T[:c, :]

    sub = lax.broadcasted_iota(jnp.int32, (SUBLANES, LANES), 0)
    zero_blk = jnp.zeros((SUBLANES, LANES), F32)
    for i in range(c):
        nblk = i // SUBLANES + 1
        acc = [zero_blk] * (nblk - 1) + [(sub == i % SUBLANES).astype(F32)]
        for j in range(i):
            a_ij = jnp.broadcast_to(at_ref[i, j:j + 1, :], (SUBLANES, LANES))
            for blk in range(j // SUBLANES + 1):
                acc[blk] = acc[blk] - a_ij * x_ref[j, blk * SUBLANES:(blk + 1) * SUBLANES, :]
        for blk in range(c // SUBLANES):
            x_ref[i, blk * SUBLANES:(blk + 1) * SUBLANES, :] = acc[blk] if blk < nblk else zero_blk

    zeros_half = jnp.zeros((LANES - c, LANES), F32)
    for i in range(c):
        mat_ref[pl.ds(i, LANES, stride=pitch), :] = jnp.concatenate([x_ref[i], zeros_half], axis=0).T

    for m, (cc, b, h) in enumerate(items):
        r0, lo = cc * c, h * DN_KEY_DIM
        _, _, beta_r, g_r = _chunk_gates(gbc_ref, gbr_ref, b, h, r0)
        t = mat_ref[m * pitch:m * pitch + c, :][:, :c]
        u_ref[b, r0:r0 + c, lo:lo + DN_VALUE_DIM] = _dot((t * beta_r).astype(BF16), v_ref[b, r0:r0 + c, lo:lo + DN_VALUE_DIM])
        w = _dot((t * (beta_r * jnp.exp(g_r))).astype(BF16), k_ref[b, r0:r0 + c, lo:lo + DN_KEY_DIM])
        w_ref[b, r0:r0 + c, lo:lo + DN_KEY_DIM] = w.astype(BF16)


def _delta_prep(dq, dk, dv, gbc, gbr, batch, seq):
    ts = DN_PREP_STEP
    tok_spec = lambda width: pl.BlockSpec((batch, ts, width), lambda i: (0, i, 0))
    return pl.pallas_call(
        _delta_prep_kernel,
        grid=(seq // ts,),
        in_specs=[
            tok_spec(DN_QK_W), tok_spec(DN_QK_W), tok_spec(DN_V_W), tok_spec(LANES),
            pl.BlockSpec((batch, SUBLANES, ts), lambda i: (0, 0, i)),
        ],
        out_specs=(
            tok_spec(DN_V_W), tok_spec(DN_QK_W), tok_spec(DN_QK_W),
            pl.BlockSpec((batch, DN_HEADS, ts, CHUNK), lambda i: (0, 0, i, 0)),
        ),
        out_shape=(
            jax.ShapeDtypeStruct((batch, seq, DN_V_W), F32),
            jax.ShapeDtypeStruct((batch, seq, DN_QK_W), BF16),
            jax.ShapeDtypeStruct((batch, seq, DN_QK_W), BF16),
            jax.ShapeDtypeStruct((batch, DN_HEADS, seq, CHUNK), BF16),
        ),
        scratch_shapes=[
            pltpu.VMEM((LANES * DN_MAT_PITCH, LANES), F32),
            pltpu.VMEM((CHUNK, CHUNK, LANES), F32),
            pltpu.VMEM((CHUNK, CHUNK, LANES), F32),
        ],
        compiler_params=pltpu.CompilerParams(dimension_semantics=("arbitrary",), vmem_limit_bytes=VMEM_LIMIT),
        name="delta_prep",
    )(dq, dk, dv, gbc, gbr)


def _delta_scan_kernel(qg_ref, kt_ref, u_ref, w_ref, attn_ref, z_ref, gbr_ref, nw_ref, o_ref, state_ref):
    step = pl.program_id(0)
    batch = qg_ref.shape[0]
    c = CHUNK
    ts = qg_ref.shape[1]
    per_tile = LANES // c
    assert ts % LANES == 0

    @pl.when(step == 0)
    def _():
        state_ref[...] = jnp.zeros_like(state_ref)

    nw = nw_ref[...]
    chains = [(b, h) for b in range(batch) for h in range(DN_HEADS)]
    lane = lax.broadcasted_iota(jnp.int32, (1, LANES), 1)

    k_dec_t, s_decay = {}, {}
    for (b, h) in chains:
        for t0 in range(0, ts, LANES):
            kt = kt_ref[b, h * DN_KEY_DIM:(h + 1) * DN_KEY_DIM, t0:t0 + LANES].astype(F32)
            g_r = gbr_ref[b, DN_HEADS + h:DN_HEADS + h + 1, t0:t0 + LANES]
            for sub in range(per_tile):
                cc = t0 // c + sub
                g_last = g_r[:, (sub + 1) * c - 1:(sub + 1) * c]
                in_chunk = (lane >= sub * c) & (lane < (sub + 1) * c)
                k_dec_t[(cc, b, h)] = (kt * jnp.exp(jnp.where(in_chunk, g_last - g_r, -1e30))).astype(BF16)
                s_decay[(cc, b, h)] = jnp.exp(g_last)

    for cc in range(ts // c):
        r0 = cc * c
        res, vb, om, kv = {}, {}, {}, {}
        for (b, h) in chains:
            lo = h * DN_KEY_DIM
            lhs = jnp.concatenate([w_ref[b, r0:r0 + c, lo:lo + DN_KEY_DIM],
                                   qg_ref[b, r0:r0 + c, lo:lo + DN_KEY_DIM]], axis=0)
            res[(b, h)] = _dot(lhs, state_ref[b * DN_HEADS + h].astype(BF16))
        for (b, h) in chains:
            lo = h * DN_VALUE_DIM
            vb[(b, h)] = (u_ref[b, r0:r0 + c, lo:lo + DN_VALUE_DIM] - res[(b, h)][:c]).astype(BF16)
        for (b, h) in chains:
            om[(b, h)] = _dot(attn_ref[b, h, r0:r0 + c, :], vb[(b, h)])
            kv[(b, h)] = _dot(k_dec_t[(cc, b, h)], jnp.concatenate([vb[(b, h)]] * per_tile, axis=0))
        for (b, h) in chains:
            idx = b * DN_HEADS + h
            state_ref[idx] = state_ref[idx] * s_decay[(cc, b, h)] + kv[(b, h)]
        for (b, h) in chains:
            lo = h * DN_VALUE_DIM
            o = res[(b, h)][c:] + om[(b, h)]
            o = o * lax.rsqrt(jnp.mean(o * o, axis=-1, keepdims=True) + RMS_EPS) * nw
            o = o * z_ref[b, r0:r0 + c, lo:lo + DN_VALUE_DIM].astype(F32)
            o_ref[b, r0:r0 + c, lo:lo + DN_VALUE_DIM] = o.astype(BF16)


def _delta_scan(qg, dkt, u, w, attn, dzs, gbr, norm_w, batch, seq):
    ts = DN_STEP
    tok_spec = lambda width: pl.BlockSpec((batch, ts, width), lambda i: (0, i, 0))
    seq_spec = lambda rows: pl.BlockSpec((batch, rows, ts), lambda i: (0, 0, i))
    return pl.pallas_call(
        _delta_scan_kernel,
        grid=(seq // ts,),
        in_specs=[
            tok_spec(DN_QK_W), seq_spec(DN_QK_W), tok_spec(DN_V_W), tok_spec(DN_QK_W),
            pl.BlockSpec((batch, DN_HEADS, ts, CHUNK), lambda i: (0, 0, i, 0)),
            tok_spec(DN_V_W), seq_spec(SUBLANES),
            _const_spec(norm_w.shape),
        ],
        out_specs=tok_spec(DN_V_W),
        out_shape=jax.ShapeDtypeStruct((batch, seq, DN_V_W), BF16),
        scratch_shapes=[pltpu.VMEM((batch * DN_HEADS, DN_KEY_DIM, DN_VALUE_DIM), F32)],
        compiler_params=pltpu.CompilerParams(dimension_semantics=("arbitrary",), vmem_limit_bytes=VMEM_LIMIT),
        name="delta_scan",
    )(qg, dkt, u, w, attn, dzs, gbr, norm_w)


def _layer_norm(x, g, b):
    mu = jnp.mean(x, axis=-1, keepdims=True)
    xc = x - mu
    var = jnp.mean(xc * xc, axis=-1, keepdims=True)
    return xc * lax.rsqrt(var + LN_EPS) * g + b


def _merge_ffn_kernel(x_ref, oa_ref, ob_ref, wg_t_ref, wao_ref, wdo_ref, wout_f32_ref, ln1_ref,
                      wup_ref, wdown_ref, ln2_ref, out_ref, wg_ref, wout_ref):
    @pl.when(pl.program_id(0) == 0)
    def _():
        for c0 in range(0, 2 * D_MODEL, LANES):
            wg_ref[:, c0:c0 + LANES] = wg_t_ref[c0:c0 + LANES, :].T.astype(BF16)
        wout_ref[...] = wout_f32_ref[...].astype(BF16)

    x = x_ref[...]
    xb = x.astype(BF16)
    merged = _sigmoid(_dot(xb, wg_ref[:, :D_MODEL])) * _dot(oa_ref[...], wao_ref[...])
    merged = merged + _sigmoid(_dot(xb, wg_ref[:, D_MODEL:])) * _dot(ob_ref[...], wdo_ref[...])
    mix = _dot(merged.astype(BF16), wout_ref[...])
    ln1 = ln1_ref[...]
    x1 = _layer_norm(DEEPNORM_ALPHA * x + mix, ln1[0:1, :], ln1[1:2, :])
    x1b = x1.astype(BF16)
    ffn = None
    for c0 in range(0, D_FF, FFN_CHUNK):
        h = jnp.maximum(_dot(x1b, wup_ref[:, c0:c0 + FFN_CHUNK]), 0.0)
        part = _dot((h * h).astype(BF16), wdown_ref[c0:c0 + FFN_CHUNK, :])
        ffn = part if ffn is None else ffn + part
    ln2 = ln2_ref[...]
    out_ref[...] = _layer_norm(DEEPNORM_ALPHA * x1 + ffn, ln2[0:1, :], ln2[1:2, :])


def _merge_ffn(x2, oa, ob, w_t, gate_row0, wao, wdo, wout_f32, ln1, wup, wdown, ln2):
    n = x2.shape[0]
    tm = FFN_TM
    row_spec = lambda width: pl.BlockSpec((tm, width), lambda i: (i, 0))
    gate_spec = pl.BlockSpec((pl.Element(2 * D_MODEL), pl.Element(D_MODEL)), lambda i: (gate_row0, 0),
                             pipeline_mode=pl.Buffered(1))
    return pl.pallas_call(
        _merge_ffn_kernel,
        grid=(n // tm,),
        in_specs=[
            row_spec(D_MODEL), row_spec(ATTN_Q_W), row_spec(DN_V_W),
            gate_spec, _const_spec(wao.shape), _const_spec(wdo.shape), _const_spec(wout_f32.shape),
            _const_spec(ln1.shape), _const_spec(wup.shape), _const_spec(wdown.shape), _const_spec(ln2.shape),
        ],
        out_specs=row_spec(D_MODEL),
        out_shape=jax.ShapeDtypeStruct((n, D_MODEL), F32),
        scratch_shapes=[pltpu.VMEM((D_MODEL, 2 * D_MODEL), BF16), pltpu.VMEM((D_MODEL, D_MODEL), BF16)],
        compiler_params=pltpu.CompilerParams(dimension_semantics=("arbitrary",), vmem_limit_bytes=VMEM_LIMIT),
        name="merge_ffn",
    )(x2, oa, ob, w_t, wao, wdo, wout_f32, ln1, wup, wdown, ln2)


def _layer(x, w_in, conv_w, attn_sinks, dn_a_log, dn_dt_bias, dn_norm_w, w_attn_out, w_dn_out, w_out,
           ln1_g, ln1_b, w_up, w_down, ln2_g, ln2_b):
    batch, seq, _ = x.shape
    n = batch * seq
    x2 = x.reshape(n, D_MODEL)

    c_gates = ATTN_Q_W + 2 * ATTN_KV_W + DN_CONV_W + DN_V_W + 2 * DN_HEADS
    w_t = jnp.swapaxes(w_in, 0, 1).astype(F32)
    gpar = jnp.zeros((2, LANES), F32)
    gpar = gpar.at[0, DN_HEADS:2 * DN_HEADS].set(dn_a_log.astype(F32))
    gpar = gpar.at[1, DN_HEADS:2 * DN_HEADS].set(dn_dt_bias.astype(F32))

    aqkv, dq, dk, dkt, dv, dzs, gbc, gbr = _project(
        x2, w_t, conv_w.astype(F32), gpar, batch, seq)
    oa = _attention(aqkv, attn_sinks.astype(F32), batch, seq)
    r3 = lambda t: t.reshape(batch, seq, t.shape[-1])
    u, w, qg, attn = _delta_prep(r3(dq), r3(dk), r3(dv), r3(gbc), gbr, batch, seq)
    ob = _delta_scan(qg, dkt, u, w, attn, r3(dzs), gbr,
                     dn_norm_w.astype(F32).reshape(1, DN_VALUE_DIM), batch, seq)
    out = _merge_ffn(
        x2, oa, ob.reshape(n, DN_V_W), w_t, c_gates, w_attn_out.astype(BF16), w_dn_out.astype(BF16),
        w_out.astype(F32), jnp.stack([ln1_g, ln1_b]).astype(F32), w_up.astype(BF16), w_down.astype(BF16),
        jnp.stack([ln2_g, ln2_b]).astype(F32))
    return out.reshape(batch, seq, D_MODEL)


def kernel(x, w_in, conv_w, attn_sinks, dn_a_log, dn_dt_bias, dn_norm_w, w_attn_out, w_dn_out, w_out,
           ln1_g, ln1_b, w_up, w_down, ln2_g, ln2_b):
    for l in range(DEPTH):
        x = _layer(x, w_in[l], conv_w[l], attn_sinks[l], dn_a_log[l], dn_dt_bias[l], dn_norm_w[l],
                   w_attn_out[l], w_dn_out[l], w_out[l], ln1_g[l], ln1_b[l], w_up[l], w_down[l],
                   ln2_g[l], ln2_b[l])
    return x
```

```python
import functools

import jax
import jax.numpy as jnp
from jax import lax
from jax.experimental import pallas as pl
from jax.experimental.pallas import tpu as pltpu

D_MODEL = 1024
ATTN_HEADS = 8
ATTN_KV_HEADS = 2
ATTN_HEAD_DIM = 64
WINDOW = 128
DN_HEADS = 4
DN_KEY_DIM = 128
DN_VALUE_DIM = 128
CONV_WIDTH = 4
CHUNK = 64
D_FF = 4 * D_MODEL
LN_EPS = 1e-5
RMS_EPS = 1e-6
DEPTH = 1
DEEPNORM_ALPHA = (2 * DEPTH) ** 0.25

ATTN_Q_W = ATTN_HEADS * ATTN_HEAD_DIM
ATTN_KV_W = ATTN_KV_HEADS * ATTN_HEAD_DIM
DN_QK_W = DN_HEADS * DN_KEY_DIM
DN_V_W = DN_HEADS * DN_VALUE_DIM
DN_CONV_W = 2 * DN_QK_W + DN_V_W

LANES = 128
SUBLANES = 8
VMEM_LIMIT = 56 * 1024 * 1024

PROJ_TM = 512
PROJ_CONV_GROUP = 2 * DN_KEY_DIM
ATTN_TQ = 512
DN_STEP = 512
DN_MAT_PITCH = CHUNK + SUBLANES
FFN_TM = 512
FFN_CHUNK = 1024

BF16 = jnp.bfloat16
F32 = jnp.float32


def _dot(a, b):
    return jnp.dot(a, b, preferred_element_type=F32)


def _dot_nt(a, b):
    return lax.dot_general(a, b, (((1,), (1,)), ((), ())), preferred_element_type=F32)


def _dot_tn(a, b):
    return lax.dot_general(a, b, (((0,), (0,)), ((), ())), preferred_element_type=F32)


def _sigmoid(x):
    return 0.5 + 0.5 * jnp.tanh(0.5 * x)


def _const_spec(shape):
    return pl.BlockSpec(shape, lambda *_: (0,) * len(shape), pipeline_mode=pl.Buffered(1))


def _proj_kernel(x_ref, w_t_ref, conv_ref, gpar_ref,
                 aqkv_ref, dq_ref, dk_ref, dkt_ref, dv_ref, dzs_ref, gbc_ref, gbr_ref,
                 w_attn_ref, w_dqkv_ref, w_dz_ref, w_ba_ref, *pbuf_refs, tiles_per_seq):
    i = pl.program_id(0)
    tm = x_ref.shape[0]
    xb = x_ref[...].astype(BF16)
    first = i % tiles_per_seq == 0
    half_cw = 0.5 * conv_ref[...]
    grp = PROJ_CONV_GROUP

    @pl.when(i == 0)
    def _():
        for pbuf_ref in pbuf_refs:
            pbuf_ref[tm:tm + SUBLANES, :] = jnp.zeros((SUBLANES, grp), F32)
        def w_cols(c0, width=LANES):
            return w_t_ref[c0:c0 + width, :].T

        lo_half = lax.broadcasted_iota(jnp.int32, (D_MODEL, LANES), 1) < ATTN_HEAD_DIM
        for c0 in range(0, ATTN_Q_W, LANES):
            w_attn_ref[:, c0:c0 + LANES] = (w_cols(c0) * (ATTN_HEAD_DIM ** -0.5)).astype(BF16)
        for n, c0 in enumerate((ATTN_Q_W, ATTN_Q_W + ATTN_KV_W)):
            pair = w_cols(c0, ATTN_KV_W)
            swapped = pltpu.roll(pair, ATTN_HEAD_DIM, 1)
            base = ATTN_Q_W + 2 * n * ATTN_KV_W
            w_attn_ref[:, base:base + LANES] = jnp.where(lo_half, pair, swapped).astype(BF16)
            w_attn_ref[:, base + LANES:base + 2 * LANES] = jnp.where(lo_half, swapped, pair).astype(BF16)
        c_dqkv = ATTN_Q_W + 2 * ATTN_KV_W
        for c0 in range(0, DN_CONV_W, LANES):
            w_dqkv_ref[:, c0:c0 + LANES] = w_cols(c_dqkv + c0).astype(BF16)
        c_dz = c_dqkv + DN_CONV_W
        for c0 in range(0, DN_V_W, LANES):
            w_dz_ref[:, c0:c0 + LANES] = w_cols(c_dz + c0).astype(BF16)
        c_ba = c_dz + DN_V_W
        keep = lax.broadcasted_iota(jnp.int32, (D_MODEL, LANES), 1) < 2 * DN_HEADS
        w_ba_ref[...] = jnp.where(keep, w_cols(c_ba), 0.0).astype(BF16)

    def silu_of_twice(h):
        return h + h * jnp.tanh(h)

    def conv_matmul(gi):
        pbuf_ref = pbuf_refs[gi]
        carry = pbuf_ref[tm:tm + SUBLANES, :]
        pbuf_ref[0:SUBLANES, :] = jnp.where(first, 0.0, carry)
        pbuf_ref[SUBLANES:SUBLANES + tm, :] = _dot(xb, w_dqkv_ref[:, gi * grp:(gi + 1) * grp])

    def conv_silu(gi):
        h = None
        for s in range(CONV_WIDTH):
            taps = half_cw[CONV_WIDTH - 1 - s:CONV_WIDTH - s, gi * grp:(gi + 1) * grp]
            term = pbuf_refs[gi][pl.ds(SUBLANES - s, tm), :] * taps
            h = term if h is None else h + term
        return silu_of_twice(h)

    def unit_rows(yh, scale):
        return yh * (lax.rsqrt(jnp.sum(yh * yh, axis=-1, keepdims=True) + RMS_EPS) * scale)

    def q_tail(c0, _):
        y = conv_silu(c0 // grp)
        for lo in range(0, grp, DN_KEY_DIM):
            dq_ref[:, c0 + lo:c0 + lo + DN_KEY_DIM] = unit_rows(y[:, lo:lo + DN_KEY_DIM], DN_KEY_DIM ** -0.5).astype(BF16)

    def k_tail(c0, _):
        y = conv_silu((DN_QK_W + c0) // grp)
        for lo in range(0, grp, DN_KEY_DIM):
            kh = unit_rows(y[:, lo:lo + DN_KEY_DIM], 1.0).astype(BF16)
            dk_ref[:, c0 + lo:c0 + lo + DN_KEY_DIM] = kh
            dkt_ref[c0 + lo:c0 + lo + DN_KEY_DIM, :] = kh.astype(F32).T.astype(BF16)

    def v_tail(c0, _):
        dv_ref[:, c0:c0 + grp] = conv_silu((2 * DN_QK_W + c0) // grp).astype(BF16)

    def attn_tail(c0, a):
        aqkv_ref[:, c0:c0 + ATTN_Q_W] = a.astype(BF16)

    def dz_tail(_, a):
        dzs_ref[...] = silu_of_twice(0.5 * a).astype(BF16)

    def gate_tail(_, ba):
        gpar = gpar_ref[...]
        lane = lax.broadcasted_iota(jnp.int32, (tm, LANES), 1)
        row = lax.broadcasted_iota(jnp.int32, (tm, LANES), 0)
        beta = _sigmoid(ba)
        z = ba + gpar[1:2, :]
        softplus = jnp.maximum(z, 0.0) + jnp.log1p(jnp.exp(-jnp.abs(z)))
        g = -jnp.exp(gpar[0:1, :]) * softplus
        s = 1
        while s < CHUNK:
            g = g + jnp.where(row % CHUNK >= s, pltpu.roll(g, s, 0), 0.0)
            s *= 2
        gb = jnp.where(lane < DN_HEADS, beta, jnp.where(lane < 2 * DN_HEADS, g, 0.0))
        gbc_ref[...] = gb
        gbr_ref[...] = gb.T[:SUBLANES, :]

    conv_stages = []
    for c0 in range(0, DN_QK_W, grp):
        conv_stages.append((functools.partial(conv_matmul, c0 // grp), q_tail, c0))
    for c0 in range(0, DN_QK_W, grp):
        conv_stages.append((functools.partial(conv_matmul, (DN_QK_W + c0) // grp), k_tail, c0))
    for c0 in range(0, DN_V_W, grp):
        conv_stages.append((functools.partial(conv_matmul, (2 * DN_QK_W + c0) // grp), v_tail, c0))
    plain_stages = []
    for c0 in range(0, w_attn_ref.shape[1], ATTN_Q_W):
        plain_stages.append((functools.partial(lambda c: _dot(xb, w_attn_ref[:, c:c + ATTN_Q_W]), c0), attn_tail, c0))
    plain_stages.append((lambda: _dot(xb, w_dz_ref[...]), dz_tail, 0))
    plain_stages.append((lambda: _dot(xb, w_ba_ref[...]), gate_tail, 0))
    stages = []
    for n, st in enumerate(conv_stages):
        stages.append(st)
        if n < len(plain_stages):
            stages.append(plain_stages[n])
    stages += plain_stages[len(conv_stages):]

    pending = stages[0][0]()
    for n, (_, tail, c0) in enumerate(stages):
        ready = pending
        if n + 1 < len(stages):
            pending = stages[n + 1][0]()
        tail(c0, ready)


def _project(x2, w_in, conv_w, gpar, batch, seq):
    n = x2.shape[0]
    tm = PROJ_TM
    tiles_per_seq = seq // tm
    attn_w = ATTN_Q_W + 4 * ATTN_KV_W
    row_spec = lambda width: pl.BlockSpec((tm, width), lambda i: (i, 0))
    seq_spec = lambda rows: pl.BlockSpec((None, rows, tm), lambda i: (i // tiles_per_seq, 0, i % tiles_per_seq))
    out_shape = (
        jax.ShapeDtypeStruct((n, attn_w), BF16),
        jax.ShapeDtypeStruct((n, DN_QK_W), BF16),
        jax.ShapeDtypeStruct((n, DN_QK_W), BF16),
        jax.ShapeDtypeStruct((batch, DN_QK_W, seq), BF16),
        jax.ShapeDtypeStruct((n, DN_V_W), BF16),
        jax.ShapeDtypeStruct((n, DN_V_W), BF16),
        jax.ShapeDtypeStruct((n, LANES), F32),
        jax.ShapeDtypeStruct((batch, SUBLANES, seq), F32),
    )
    return pl.pallas_call(
        functools.partial(_proj_kernel, tiles_per_seq=tiles_per_seq),
        grid=(n // tm,),
        in_specs=[
            row_spec(D_MODEL),
            _const_spec(w_in.shape), _const_spec(conv_w.shape), _const_spec(gpar.shape),
        ],
        out_specs=(
            row_spec(attn_w), row_spec(DN_QK_W), row_spec(DN_QK_W), seq_spec(DN_QK_W),
            row_spec(DN_V_W), row_spec(DN_V_W), row_spec(LANES), seq_spec(SUBLANES),
        ),
        out_shape=out_shape,
        scratch_shapes=[pltpu.VMEM((D_MODEL, attn_w), BF16), pltpu.VMEM((D_MODEL, DN_CONV_W), BF16),
                        pltpu.VMEM((D_MODEL, DN_V_W), BF16), pltpu.VMEM((D_MODEL, LANES), BF16)]
        + [pltpu.VMEM((SUBLANES + tm, PROJ_CONV_GROUP), F32)] * (DN_CONV_W // PROJ_CONV_GROUP),
        compiler_params=pltpu.CompilerParams(dimension_semantics=("arbitrary",), vmem_limit_bytes=VMEM_LIMIT),
        name="proj_dnprep",
    )(x2, w_in, conv_w, gpar)


def _attn_kernel(sink_ref, q_ref, kv_ref, kvp_ref, o_ref):
    j = pl.program_id(1)
    tq = q_ref.shape[0]
    w = WINDOW
    pair_w = 2 * ATTN_HEAD_DIM
    group = ATTN_HEADS // ATTN_KV_HEADS
    lo = lax.broadcasted_iota(jnp.int32, (w, pair_w), 1) < ATTN_HEAD_DIM
    qi = lax.broadcasted_iota(jnp.int32, (w, 2 * w), 0)
    kj = lax.broadcasted_iota(jnp.int32, (w, 2 * w), 1)
    band = (kj > qi) & (kj <= qi + w)
    zero = jnp.zeros((), BF16)

    def two_windows(wi, col):
        cols = slice(col * pair_w, (col + 1) * pair_w)
        prev = kvp_ref[:, cols] if wi == 0 else kv_ref[(wi - 1) * w:wi * w, cols]
        return jnp.concatenate([prev, kv_ref[wi * w:(wi + 1) * w, cols]], axis=0)

    def scores(wi, hk):
        parts = []
        for pp in range(group // 2):
            pair = hk * (group // 2) + pp
            qp = q_ref[wi * w:(wi + 1) * w, pair * pair_w:(pair + 1) * pair_w]
            parts += [jnp.where(lo, qp, zero), jnp.where(lo, zero, qp)]
        return _dot_nt(jnp.concatenate(parts, axis=0), two_windows(wi, hk))

    def finish(wi, hk, s):
        mask = band & ((kj >= w) | (j > 0)) if wi == 0 else band
        ps, invs = [], []
        for e in range(group):
            se = jnp.where(mask, s[e * w:(e + 1) * w], -1e30)
            sink = sink_ref[hk * group + e]
            m = jnp.maximum(jnp.max(se, axis=-1, keepdims=True), sink)
            pe = jnp.exp(se - m)
            den = jnp.sum(pe, axis=-1, keepdims=True) + jnp.exp(sink - m)
            ps.append(pe.astype(BF16))
            invs.append(1.0 / den)
        o = _dot(jnp.concatenate(ps, axis=0), two_windows(wi, ATTN_KV_HEADS + hk))
        for pp in range(group // 2):
            pair = hk * (group // 2) + pp
            oa = o[(2 * pp) * w:(2 * pp + 1) * w] * invs[2 * pp]
            ob = o[(2 * pp + 1) * w:(2 * pp + 2) * w] * invs[2 * pp + 1]
            o_ref[wi * w:(wi + 1) * w, pair * pair_w:(pair + 1) * pair_w] = jnp.where(lo, oa, ob).astype(BF16)

    units = [(wi, hk) for wi in range(tq // w) for hk in range(ATTN_KV_HEADS)]
    s_next = scores(*units[0])
    for n, unit in enumerate(units):
        s = s_next
        if n + 1 < len(units):
            s_next = scores(*units[n + 1])
        finish(*unit, s)


def _attention(aqkv, sinks, batch, seq):
    n = aqkv.shape[0]
    tq = ATTN_TQ
    nq = seq // tq
    wpt = tq // WINDOW
    kv_w = aqkv.shape[1] - ATTN_Q_W
    assert kv_w == ATTN_Q_W
    return pl.pallas_call(
        _attn_kernel,
        grid=(batch, nq),
        in_specs=[
            pl.BlockSpec(memory_space=pltpu.SMEM),
            pl.BlockSpec((tq, ATTN_Q_W), lambda b, j: (b * nq + j, 0)),
            pl.BlockSpec((tq, kv_w), lambda b, j: (b * nq + j, 1)),
            pl.BlockSpec((WINDOW, kv_w), lambda b, j: (jnp.maximum((b * nq + j) * wpt - 1, 0), 1)),
        ],
        out_specs=pl.BlockSpec((tq, ATTN_Q_W), lambda b, j: (b * nq + j, 0)),
        out_shape=jax.ShapeDtypeStruct((n, ATTN_Q_W), BF16),
        compiler_params=pltpu.CompilerParams(dimension_semantics=("arbitrary", "arbitrary"),
                                             vmem_limit_bytes=VMEM_LIMIT),
        name="swa_sinks",
    )(sinks, aqkv, aqkv, aqkv)


def _chunk_gates(gbc_ref, gbr_ref, b, h, r0):
    c = CHUNK
    beta_c = gbc_ref[b, r0:r0 + c, h:h + 1]
    g_c = gbc_ref[b, r0:r0 + c, DN_HEADS + h:DN_HEADS + h + 1]
    beta_r = gbr_ref[b, h:h + 1, r0:r0 + c]
    g_r = gbr_ref[b, DN_HEADS + h:DN_HEADS + h + 1, r0:r0 + c]
    return beta_c, g_c, beta_r, g_r


def _delta_prep_part(q_ref, k_ref, v_ref, gbc_ref, gbr_ref, u_ref, w_ref, qg_ref, attn_ref,
                     mat_ref, at_ref, x_ref):
    batch = q_ref.shape[0]
    c = CHUNK
    n_chunks = q_ref.shape[1] // c
    items = [(cc, b, h) for cc in range(n_chunks) for b in range(batch) for h in range(DN_HEADS)]
    assert len(items) == LANES and c * 2 == LANES
    pitch = DN_MAT_PITCH

    row = lax.broadcasted_iota(jnp.int32, (c, LANES), 0)
    col = lax.broadcasted_iota(jnp.int32, (c, LANES), 1)
    causal = row >= col
    strict = row > col

    for m, (cc, b, h) in enumerate(items):
        r0, lo = cc * c, h * DN_KEY_DIM
        k = k_ref[b, r0:r0 + c, lo:lo + DN_KEY_DIM]
        q = q_ref[b, r0:r0 + c, lo:lo + DN_KEY_DIM]
        s = _dot_nt(jnp.concatenate([q, k], axis=0), jnp.concatenate([k, k], axis=0))
        beta_c, g_c, _, g_r = _chunk_gates(gbc_ref, gbr_ref, b, h, r0)
        g_cb = jnp.broadcast_to(g_c, (c, LANES))
        qg_ref[b, r0:r0 + c, lo:lo + DN_KEY_DIM] = (q.astype(F32) * jnp.exp(g_cb)).astype(BF16)
        g_r2 = jnp.concatenate([g_r, g_r], axis=1)
        decay = jnp.exp(jnp.where(causal, g_cb - g_r2, -1e30))
        attn_ref[b, h, r0:r0 + c, :] = (s[:c] * decay)[:, :c].astype(BF16)
        mat_ref[m * pitch:m * pitch + c, :] = jnp.where(strict, beta_c * s[c:] * decay, 0.0)

    for i in range(1, c):
        at_ref[i] = mat_ref[pl.ds(i, LANES, stride=pitch), :].T[:c, :]

    sub = lax.broadcasted_iota(jnp.int32, (SUBLANES, LANES), 0)
    zero_blk = jnp.zeros((SUBLANES, LANES), F32)
    for i in range(c):
        nblk = i // SUBLANES + 1
        acc = [zero_blk] * (nblk - 1) + [(sub == i % SUBLANES).astype(F32)]
        for j in range(i):
            a_ij = jnp.broadcast_to(at_ref[i, j:j + 1, :], (SUBLANES, LANES))
            for blk in range(j // SUBLANES + 1):
                acc[blk] = acc[blk] - a_ij * x_ref[j, blk * SUBLANES:(blk + 1) * SUBLANES, :]
        for blk in range(c // SUBLANES):
            x_ref[i, blk * SUBLANES:(blk + 1) * SUBLANES, :] = acc[blk] if blk < nblk else zero_blk

    zeros_half = jnp.zeros((LANES - c, LANES), F32)
    for i in range(c):
        mat_ref[pl.ds(i, LANES, stride=pitch), :] = jnp.concatenate([x_ref[i], zeros_half], axis=0).T

    for m, (cc, b, h) in enumerate(items):
        r0, lo = cc * c, h * DN_KEY_DIM
        _, _, beta_r, g_r = _chunk_gates(gbc_ref, gbr_ref, b, h, r0)
        t = mat_ref[m * pitch:m * pitch + c, :][:, :c]
        u_ref[b, r0:r0 + c, lo:lo + DN_VALUE_DIM] = _dot((t * beta_r).astype(BF16), v_ref[b, r0:r0 + c, lo:lo + DN_VALUE_DIM])
        w = _dot((t * (beta_r * jnp.exp(g_r))).astype(BF16), k_ref[b, r0:r0 + c, lo:lo + DN_KEY_DIM])
        w_ref[b, r0:r0 + c, lo:lo + DN_KEY_DIM] = w.astype(BF16)


def _delta_scan_part(qg_ref, kt_ref, u_ref, w_ref, attn_ref, z_ref, gbr_ref, nw_ref, o_ref, state_ref):
    step = pl.program_id(0)
    batch = qg_ref.shape[0]
    c = CHUNK
    ts = qg_ref.shape[1]
    per_tile = LANES // c
    assert ts % LANES == 0

    @pl.when(step == 0)
    def _():
        state_ref[...] = jnp.zeros_like(state_ref)

    nw = nw_ref[...]
    chains = [(b, h) for b in range(batch) for h in range(DN_HEADS)]
    lane = lax.broadcasted_iota(jnp.int32, (1, LANES), 1)

    k_dec_t, s_decay = {}, {}
    for (b, h) in chains:
        for t0 in range(0, ts, LANES):
            kt = kt_ref[b, h * DN_KEY_DIM:(h + 1) * DN_KEY_DIM, t0:t0 + LANES].astype(F32)
            g_r = gbr_ref[b, DN_HEADS + h:DN_HEADS + h + 1, t0:t0 + LANES]
            for sub in range(per_tile):
                cc = t0 // c + sub
                g_last = g_r[:, (sub + 1) * c - 1:(sub + 1) * c]
                in_chunk = (lane >= sub * c) & (lane < (sub + 1) * c)
                k_dec_t[(cc, b, h)] = (kt * jnp.exp(jnp.where(in_chunk, g_last - g_r, -1e30))).astype(BF16)
                s_decay[(cc, b, h)] = jnp.exp(g_last)

    def finish_outputs(cc, q_state, om):
        r0 = cc * c
        for (b, h) in chains:
            lo = h * DN_VALUE_DIM
            o = q_state[(b, h)] + om[(b, h)]
            o = o * lax.rsqrt(jnp.mean(o * o, axis=-1, keepdims=True) + RMS_EPS) * nw
            o = o * z_ref[b, r0:r0 + c, lo:lo + DN_VALUE_DIM].astype(F32)
            o_ref[b, r0:r0 + c, lo:lo + DN_VALUE_DIM] = o.astype(BF16)

    unfinished = None
    for cc in range(ts // c):
        r0 = cc * c
        res, vb, om, kv = {}, {}, {}, {}
        for (b, h) in chains:
            lo = h * DN_KEY_DIM
            lhs = jnp.concatenate([w_ref[b, r0:r0 + c, lo:lo + DN_KEY_DIM],
                                   qg_ref[b, r0:r0 + c, lo:lo + DN_KEY_DIM]], axis=0)
            res[(b, h)] = _dot(lhs, state_ref[b * DN_HEADS + h].astype(BF16))
        if unfinished is not None:
            finish_outputs(*unfinished)
        for (b, h) in chains:
            lo = h * DN_VALUE_DIM
            vb[(b, h)] = (u_ref[b, r0:r0 + c, lo:lo + DN_VALUE_DIM] - res[(b, h)][:c]).astype(BF16)
        for (b, h) in chains:
            om[(b, h)] = _dot(attn_ref[b, h, r0:r0 + c, :], vb[(b, h)])
            kv[(b, h)] = _dot(k_dec_t[(cc, b, h)], jnp.concatenate([vb[(b, h)]] * per_tile, axis=0))
        for (b, h) in chains:
            idx = b * DN_HEADS + h
            state_ref[idx] = state_ref[idx] * s_decay[(cc, b, h)] + kv[(b, h)]
        unfinished = (cc, {bh: r[c:] for bh, r in res.items()}, om)
    finish_outputs(*unfinished)


def _delta_kernel(q_ref, k_ref, v_ref, kt_ref, z_ref, gbc_ref, gbr_ref, nw_ref, o_ref,
                  mat_ref, at_ref, x_ref, u_ref, w_ref, qg_ref, attn_ref, state_ref):
    _delta_prep_part(q_ref, k_ref, v_ref, gbc_ref, gbr_ref, u_ref, w_ref, qg_ref, attn_ref, mat_ref, at_ref, x_ref)
    _delta_scan_part(qg_ref, kt_ref, u_ref, w_ref, attn_ref, z_ref, gbr_ref, nw_ref, o_ref, state_ref)


def _delta_rule(dq, dk, dv, dkt, dzs, gbc, gbr, norm_w, batch, seq):
    ts = DN_STEP
    tok_spec = lambda width: pl.BlockSpec((batch, ts, width), lambda i: (0, i, 0))
    seq_spec = lambda rows: pl.BlockSpec((batch, rows, ts), lambda i: (0, 0, i))
    return pl.pallas_call(
        _delta_kernel,
        grid=(seq // ts,),
        in_specs=[
            tok_spec(DN_QK_W), tok_spec(DN_QK_W), tok_spec(DN_V_W), seq_spec(DN_QK_W), tok_spec(DN_V_W),
            tok_spec(LANES), seq_spec(SUBLANES), _const_spec(norm_w.shape),
        ],
        out_specs=tok_spec(DN_V_W),
        out_shape=jax.ShapeDtypeStruct((batch, seq, DN_V_W), BF16),
        scratch_shapes=[
            pltpu.VMEM((LANES * DN_MAT_PITCH, LANES), F32),
            pltpu.VMEM((CHUNK, CHUNK, LANES), F32),
            pltpu.VMEM((CHUNK, CHUNK, LANES), F32),
            pltpu.VMEM((batch, ts, DN_V_W), F32),
            pltpu.VMEM((batch, ts, DN_QK_W), BF16),
            pltpu.VMEM((batch, ts, DN_QK_W), BF16),
            pltpu.VMEM((batch, DN_HEADS, ts, CHUNK), BF16),
            pltpu.VMEM((batch * DN_HEADS, DN_KEY_DIM, DN_VALUE_DIM), F32),
        ],
        compiler_params=pltpu.CompilerParams(dimension_semantics=("arbitrary",), vmem_limit_bytes=VMEM_LIMIT),
        name="gated_delta_rule",
    )(dq, dk, dv, dkt, dzs, gbc, gbr, norm_w)


def _layer_norm(x, g, b):
    mu = jnp.mean(x, axis=-1, keepdims=True)
    xc = x - mu
    var = jnp.mean(xc * xc, axis=-1, keepdims=True)
    return xc * lax.rsqrt(var + LN_EPS) * g + b


def _merge_ffn_kernel(x_ref, oa_ref, ob_ref, wg_t_ref, wao_ref, wdo_ref, wout_f32_ref, ln1_ref,
                      wup_ref, wdown_ref, ln2_ref, out_ref, wg_ref, wout_ref):
    @pl.when(pl.program_id(0) == 0)
    def _():
        for c0 in range(0, 2 * D_MODEL, LANES):
            wg_ref[:, c0:c0 + LANES] = wg_t_ref[c0:c0 + LANES, :].T.astype(BF16)
        wout_ref[...] = wout_f32_ref[...].astype(BF16)

    x = x_ref[...]
    xb = x.astype(BF16)
    merged = _sigmoid(_dot(xb, wg_ref[:, :D_MODEL])) * _dot(oa_ref[...], wao_ref[...])
    merged = merged + _sigmoid(_dot(xb, wg_ref[:, D_MODEL:])) * _dot(ob_ref[...], wdo_ref[...])
    mix = _dot(merged.astype(BF16), wout_ref[...])
    ln1 = ln1_ref[...]
    x1 = _layer_norm(DEEPNORM_ALPHA * x + mix, ln1[0:1, :], ln1[1:2, :])
    x1b = x1.astype(BF16)
    ffn = None
    for c0 in range(0, D_FF, FFN_CHUNK):
        h = jnp.maximum(_dot(x1b, wup_ref[:, c0:c0 + FFN_CHUNK]), 0.0)
        part = _dot((h * h).astype(BF16), wdown_ref[c0:c0 + FFN_CHUNK, :])
        ffn = part if ffn is None else ffn + part
    ln2 = ln2_ref[...]
    out_ref[...] = _layer_norm(DEEPNORM_ALPHA * x1 + ffn, ln2[0:1, :], ln2[1:2, :])


def _merge_ffn(x2, oa, ob, w_t, gate_row0, wao, wdo, wout_f32, ln1, wup, wdown, ln2):
    n = x2.shape[0]
    tm = FFN_TM
    row_spec = lambda width: pl.BlockSpec((tm, width), lambda i: (i, 0))
    gate_spec = pl.BlockSpec((pl.Element(2 * D_MODEL), pl.Element(D_MODEL)), lambda i: (gate_row0, 0),
                             pipeline_mode=pl.Buffered(1))
    return pl.pallas_call(
        _merge_ffn_kernel,
        grid=(n // tm,),
        in_specs=[
            row_spec(D_MODEL), row_spec(ATTN_Q_W), row_spec(DN_V_W),
            gate_spec, _const_spec(wao.shape), _const_spec(wdo.shape), _const_spec(wout_f32.shape),
            _const_spec(ln1.shape), _const_spec(wup.shape), _const_spec(wdown.shape), _const_spec(ln2.shape),
        ],
        out_specs=row_spec(D_MODEL),
        out_shape=jax.ShapeDtypeStruct((n, D_MODEL), F32),
        scratch_shapes=[pltpu.VMEM((D_MODEL, 2 * D_MODEL), BF16), pltpu.VMEM((D_MODEL, D_MODEL), BF16)],
        compiler_params=pltpu.CompilerParams(dimension_semantics=("arbitrary",), vmem_limit_bytes=VMEM_LIMIT),
        name="merge_ffn",
    )(x2, oa, ob, w_t, wao, wdo, wout_f32, ln1, wup, wdown, ln2)


def _layer(x, w_in, conv_w, attn_sinks, dn_a_log, dn_dt_bias, dn_norm_w, w_attn_out, w_dn_out, w_out,
           ln1_g, ln1_b, w_up, w_down, ln2_g, ln2_b):
    batch, seq, _ = x.shape
    n = batch * seq
    x2 = x.reshape(n, D_MODEL)

    c_gates = ATTN_Q_W + 2 * ATTN_KV_W + DN_CONV_W + DN_V_W + 2 * DN_HEADS
    w_t = jnp.swapaxes(w_in, 0, 1).astype(F32)
    gpar = jnp.zeros((2, LANES), F32)
    gpar = gpar.at[0, DN_HEADS:2 * DN_HEADS].set(dn_a_log.astype(F32))
    gpar = gpar.at[1, DN_HEADS:2 * DN_HEADS].set(dn_dt_bias.astype(F32))

    aqkv, dq, dk, dkt, dv, dzs, gbc, gbr = _project(
        x2, w_t, conv_w.astype(F32), gpar, batch, seq)
    oa = _attention(aqkv, attn_sinks.astype(F32), batch, seq)
    r3 = lambda t: t.reshape(batch, seq, t.shape[-1])
    ob = _delta_rule(r3(dq), r3(dk), r3(dv), dkt, r3(dzs), r3(gbc), gbr,
                     dn_norm_w.astype(F32).reshape(1, DN_VALUE_DIM), batch, seq)
    out = _merge_ffn(
        x2, oa, ob.reshape(n, DN_V_W), w_t, c_gates, w_attn_out.astype(BF16), w_dn_out.astype(BF16),
        w_out.astype(F32), jnp.stack([ln1_g, ln1_b]).astype(F32), w_up.astype(BF16), w_down.astype(BF16),
        jnp.stack([ln2_g, ln2_b]).astype(F32))
    return out.reshape(batch, seq, D_MODEL)


def kernel(x, w_in, conv_w, attn_sinks, dn_a_log, dn_dt_bias, dn_norm_w, w_attn_out, w_dn_out, w_out,
           ln1_g, ln1_b, w_up, w_down, ln2_g, ln2_b):
    for l in range(DEPTH):
        x = _layer(x, w_in[l], conv_w[l], attn_sinks[l], dn_a_log[l], dn_dt_bias[l], dn_norm_w[l],
                   w_attn_out[l], w_dn_out[l], w_out[l], ln1_g[l], ln1_b[l], w_up[l], w_down[l],
                   ln2_g[l], ln2_b[l])
    return x
```

```python
import functools

import jax
import jax.numpy as jnp
from jax import lax
from jax.experimental import pallas as pl
from jax.experimental.pallas import tpu as pltpu

D_MODEL = 1024
ATTN_HEADS = 8
ATTN_KV_HEADS = 2
ATTN_HEAD_DIM = 64
WINDOW = 128
DN_HEADS = 4
DN_KEY_DIM = 128
DN_VALUE_DIM = 128
CONV_WIDTH = 4
CHUNK = 64
D_FF = 4 * D_MODEL
LN_EPS = 1e-5
RMS_EPS = 1e-6
DEPTH = 1
DEEPNORM_ALPHA = (2 * DEPTH) ** 0.25

ATTN_Q_W = ATTN_HEADS * ATTN_HEAD_DIM
ATTN_KV_W = ATTN_KV_HEADS * ATTN_HEAD_DIM
DN_QK_W = DN_HEADS * DN_KEY_DIM
DN_V_W = DN_HEADS * DN_VALUE_DIM
DN_CONV_W = 2 * DN_QK_W + DN_V_W

LANES = 128
SUBLANES = 8
VMEM_LIMIT = 56 * 1024 * 1024

PROJ_TM = 512
PROJ_CONV_GROUP = 2 * DN_KEY_DIM
ATTN_TQ = 2048
DN_STEP = 512
DN_MAT_PITCH = CHUNK + SUBLANES
FFN_TM = 512
FFN_CHUNK = 1024

BF16 = jnp.bfloat16
F32 = jnp.float32


def _dot(a, b):
    return jnp.dot(a, b, preferred_element_type=F32)


def _dot_nt(a, b):
    return lax.dot_general(a, b, (((1,), (1,)), ((), ())), preferred_element_type=F32)


def _dot_tn(a, b):
    return lax.dot_general(a, b, (((0,), (0,)), ((), ())), preferred_element_type=F32)


def _sigmoid(x):
    return 0.5 + 0.5 * jnp.tanh(0.5 * x)


def _const_spec(shape):
    return pl.BlockSpec(shape, lambda *_: (0,) * len(shape), pipeline_mode=pl.Buffered(1))


def _proj_kernel(x_ref, w_t_ref, conv_ref, gpar_ref,
                 aqkv_ref, dq_ref, dk_ref, dkt_ref, dv_ref, dzs_ref, gbc_ref, gbr_ref,
                 w_attn_ref, w_dqkv_ref, w_dz_ref, w_ba_ref, *pbuf_refs, tiles_per_seq):
    i = pl.program_id(0)
    tm = x_ref.shape[0]
    xb = x_ref[...].astype(BF16)
    first = i % tiles_per_seq == 0
    half_cw = 0.5 * conv_ref[...]
    grp = PROJ_CONV_GROUP

    @pl.when(i == 0)
    def _():
        for pbuf_ref in pbuf_refs:
            pbuf_ref[tm:tm + SUBLANES, :] = jnp.zeros((SUBLANES, grp), F32)
        def w_cols(c0, width=LANES):
            return w_t_ref[c0:c0 + width, :].T

        lo_half = lax.broadcasted_iota(jnp.int32, (D_MODEL, LANES), 1) < ATTN_HEAD_DIM
        for c0 in range(0, ATTN_Q_W, LANES):
            w_attn_ref[:, c0:c0 + LANES] = (w_cols(c0) * (ATTN_HEAD_DIM ** -0.5)).astype(BF16)
        for n, c0 in enumerate((ATTN_Q_W, ATTN_Q_W + ATTN_KV_W)):
            pair = w_cols(c0, ATTN_KV_W)
            swapped = pltpu.roll(pair, ATTN_HEAD_DIM, 1)
            base = ATTN_Q_W + 2 * n * ATTN_KV_W
            w_attn_ref[:, base:base + LANES] = jnp.where(lo_half, pair, swapped).astype(BF16)
            w_attn_ref[:, base + LANES:base + 2 * LANES] = jnp.where(lo_half, swapped, pair).astype(BF16)
        c_dqkv = ATTN_Q_W + 2 * ATTN_KV_W
        for c0 in range(0, DN_CONV_W, LANES):
            w_dqkv_ref[:, c0:c0 + LANES] = w_cols(c_dqkv + c0).astype(BF16)
        c_dz = c_dqkv + DN_CONV_W
        for c0 in range(0, DN_V_W, LANES):
            w_dz_ref[:, c0:c0 + LANES] = w_cols(c_dz + c0).astype(BF16)
        c_ba = c_dz + DN_V_W
        keep = lax.broadcasted_iota(jnp.int32, (D_MODEL, LANES), 1) < 2 * DN_HEADS
        w_ba_ref[...] = jnp.where(keep, w_cols(c_ba), 0.0).astype(BF16)

    def silu_of_twice(h):
        return h + h * jnp.tanh(h)

    def conv_matmul(gi):
        pbuf_ref = pbuf_refs[gi]
        carry = pbuf_ref[tm:tm + SUBLANES, :]
        pbuf_ref[0:SUBLANES, :] = jnp.where(first, 0.0, carry)
        pbuf_ref[SUBLANES:SUBLANES + tm, :] = _dot(xb, w_dqkv_ref[:, gi * grp:(gi + 1) * grp])

    def conv_silu(gi):
        h = None
        for s in range(CONV_WIDTH):
            taps = half_cw[CONV_WIDTH - 1 - s:CONV_WIDTH - s, gi * grp:(gi + 1) * grp]
            term = pbuf_refs[gi][pl.ds(SUBLANES - s, tm), :] * taps
            h = term if h is None else h + term
        return silu_of_twice(h)

    def unit_rows(yh, scale):
        return yh * (lax.rsqrt(jnp.sum(yh * yh, axis=-1, keepdims=True) + RMS_EPS) * scale)

    def q_tail(c0, _):
        y = conv_silu(c0 // grp)
        for lo in range(0, grp, DN_KEY_DIM):
            dq_ref[:, c0 + lo:c0 + lo + DN_KEY_DIM] = unit_rows(y[:, lo:lo + DN_KEY_DIM], DN_KEY_DIM ** -0.5).astype(BF16)

    def k_tail(c0, _):
        y = conv_silu((DN_QK_W + c0) // grp)
        for lo in range(0, grp, DN_KEY_DIM):
            kh = unit_rows(y[:, lo:lo + DN_KEY_DIM], 1.0).astype(BF16)
            dk_ref[:, c0 + lo:c0 + lo + DN_KEY_DIM] = kh
            dkt_ref[c0 + lo:c0 + lo + DN_KEY_DIM, :] = kh.astype(F32).T.astype(BF16)

    def v_tail(c0, _):
        dv_ref[:, c0:c0 + grp] = conv_silu((2 * DN_QK_W + c0) // grp).astype(BF16)

    def attn_tail(c0, a):
        aqkv_ref[:, c0:c0 + ATTN_Q_W] = a.astype(BF16)

    def dz_tail(_, a):
        dzs_ref[...] = silu_of_twice(0.5 * a).astype(BF16)

    def gate_tail(_, ba):
        gpar = gpar_ref[...]
        lane = lax.broadcasted_iota(jnp.int32, (tm, LANES), 1)
        row = lax.broadcasted_iota(jnp.int32, (tm, LANES), 0)
        beta = _sigmoid(ba)
        z = ba + gpar[1:2, :]
        softplus = jnp.maximum(z, 0.0) + jnp.log1p(jnp.exp(-jnp.abs(z)))
        g = -jnp.exp(gpar[0:1, :]) * softplus
        s = 1
        while s < CHUNK:
            g = g + jnp.where(row % CHUNK >= s, pltpu.roll(g, s, 0), 0.0)
            s *= 2
        gb = jnp.where(lane < DN_HEADS, beta, jnp.where(lane < 2 * DN_HEADS, g, 0.0))
        gbc_ref[...] = gb
        gbr_ref[...] = gb.T[:SUBLANES, :]

    conv_stages = []
    for c0 in range(0, DN_QK_W, grp):
        conv_stages.append((functools.partial(conv_matmul, c0 // grp), q_tail, c0))
    for c0 in range(0, DN_QK_W, grp):
        conv_stages.append((functools.partial(conv_matmul, (DN_QK_W + c0) // grp), k_tail, c0))
    for c0 in range(0, DN_V_W, grp):
        conv_stages.append((functools.partial(conv_matmul, (2 * DN_QK_W + c0) // grp), v_tail, c0))
    plain_stages = []
    for c0 in range(0, w_attn_ref.shape[1], ATTN_Q_W):
        plain_stages.append((functools.partial(lambda c: _dot(xb, w_attn_ref[:, c:c + ATTN_Q_W]), c0), attn_tail, c0))
    plain_stages.append((lambda: _dot(xb, w_dz_ref[...]), dz_tail, 0))
    plain_stages.append((lambda: _dot(xb, w_ba_ref[...]), gate_tail, 0))
    stages = []
    for n, st in enumerate(conv_stages):
        stages.append(st)
        if n < len(plain_stages):
            stages.append(plain_stages[n])
    stages += plain_stages[len(conv_stages):]

    pending = stages[0][0]()
    for n, (_, tail, c0) in enumerate(stages):
        ready = pending
        if n + 1 < len(stages):
            pending = stages[n + 1][0]()
        tail(c0, ready)


def _project(x2, w_t, conv_w, gpar, batch, seq):
    n = x2.shape[0]
    tm = PROJ_TM
    tiles_per_seq = seq // tm
    attn_w = ATTN_Q_W + 4 * ATTN_KV_W
    row_spec = lambda width: pl.BlockSpec((tm, width), lambda i: (i, 0))
    seq_spec = lambda rows: pl.BlockSpec((None, rows, tm), lambda i: (i // tiles_per_seq, 0, i % tiles_per_seq))
    out_shape = (
        jax.ShapeDtypeStruct((n, attn_w), BF16),
        jax.ShapeDtypeStruct((n, DN_QK_W), BF16),
        jax.ShapeDtypeStruct((n, DN_QK_W), BF16),
        jax.ShapeDtypeStruct((batch, DN_QK_W, seq), BF16),
        jax.ShapeDtypeStruct((n, DN_V_W), BF16),
        jax.ShapeDtypeStruct((n, DN_V_W), BF16),
        jax.ShapeDtypeStruct((n, LANES), F32),
        jax.ShapeDtypeStruct((batch, SUBLANES, seq), F32),
    )
    return pl.pallas_call(
        functools.partial(_proj_kernel, tiles_per_seq=tiles_per_seq),
        grid=(n // tm,),
        in_specs=[
            row_spec(D_MODEL),
            _const_spec(w_t.shape), _const_spec(conv_w.shape), _const_spec(gpar.shape),
        ],
        out_specs=(
            row_spec(attn_w), row_spec(DN_QK_W), row_spec(DN_QK_W), seq_spec(DN_QK_W),
            row_spec(DN_V_W), row_spec(DN_V_W), row_spec(LANES), seq_spec(SUBLANES),
        ),
        out_shape=out_shape,
        scratch_shapes=[pltpu.VMEM((D_MODEL, attn_w), BF16), pltpu.VMEM((D_MODEL, DN_CONV_W), BF16),
                        pltpu.VMEM((D_MODEL, DN_V_W), BF16), pltpu.VMEM((D_MODEL, LANES), BF16)]
        + [pltpu.VMEM((SUBLANES + tm, PROJ_CONV_GROUP), F32)] * (DN_CONV_W // PROJ_CONV_GROUP),
        compiler_params=pltpu.CompilerParams(dimension_semantics=("arbitrary",), vmem_limit_bytes=VMEM_LIMIT),
        name="proj_dnprep",
    )(x2, w_t, conv_w, gpar)


def _attn_kernel(sink_ref, q_ref, kv_ref, kvp_ref, o_ref):
    j = pl.program_id(1)
    tq = q_ref.shape[0]
    w = WINDOW
    pair_w = 2 * ATTN_HEAD_DIM
    group = ATTN_HEADS // ATTN_KV_HEADS
    lo = lax.broadcasted_iota(jnp.int32, (w, pair_w), 1) < ATTN_HEAD_DIM
    qi = lax.broadcasted_iota(jnp.int32, (w, 2 * w), 0)
    kj = lax.broadcasted_iota(jnp.int32, (w, 2 * w), 1)
    band = (kj > qi) & (kj <= qi + w)
    zero = jnp.zeros((), BF16)

    def two_windows(wi, col):
        cols = slice(col * pair_w, (col + 1) * pair_w)
        prev = kvp_ref[:, cols] if wi == 0 else kv_ref[(wi - 1) * w:wi * w, cols]
        return jnp.concatenate([prev, kv_ref[wi * w:(wi + 1) * w, cols]], axis=0)

    def scores(wi, hk):
        parts = []
        for pp in range(group // 2):
            pair = hk * (group // 2) + pp
            qp = q_ref[wi * w:(wi + 1) * w, pair * pair_w:(pair + 1) * pair_w]
            parts += [jnp.where(lo, qp, zero), jnp.where(lo, zero, qp)]
        return _dot_nt(jnp.concatenate(parts, axis=0), two_windows(wi, hk))

    def finish(wi, hk, s):
        mask = band & ((kj >= w) | (j > 0)) if wi == 0 else band
        ps, invs = [], []
        for e in range(group):
            se = jnp.where(mask, s[e * w:(e + 1) * w], -1e30)
            sink = sink_ref[hk * group + e]
            m = jnp.maximum(jnp.max(se, axis=-1, keepdims=True), sink)
            pe = jnp.exp(se - m)
            den = jnp.sum(pe, axis=-1, keepdims=True) + jnp.exp(sink - m)
            ps.append(pe.astype(BF16))
            invs.append(1.0 / den)
        o = _dot(jnp.concatenate(ps, axis=0), two_windows(wi, ATTN_KV_HEADS + hk))
        for pp in range(group // 2):
            pair = hk * (group // 2) + pp
            oa = o[(2 * pp) * w:(2 * pp + 1) * w] * invs[2 * pp]
            ob = o[(2 * pp + 1) * w:(2 * pp + 2) * w] * invs[2 * pp + 1]
            o_ref[wi * w:(wi + 1) * w, pair * pair_w:(pair + 1) * pair_w] = jnp.where(lo, oa, ob).astype(BF16)

    units = [(wi, hk) for wi in range(tq // w) for hk in range(ATTN_KV_HEADS)]
    s_next = scores(*units[0])
    for n, unit in enumerate(units):
        s = s_next
        if n + 1 < len(units):
            s_next = scores(*units[n + 1])
        finish(*unit, s)


def _attention(aqkv, sinks, batch, seq):
    n = aqkv.shape[0]
    tq = ATTN_TQ
    nq = seq // tq
    wpt = tq // WINDOW
    kv_w = aqkv.shape[1] - ATTN_Q_W
    assert kv_w == ATTN_Q_W
    return pl.pallas_call(
        _attn_kernel,
        grid=(batch, nq),
        in_specs=[
            pl.BlockSpec(memory_space=pltpu.SMEM),
            pl.BlockSpec((tq, ATTN_Q_W), lambda b, j: (b * nq + j, 0)),
            pl.BlockSpec((tq, kv_w), lambda b, j: (b * nq + j, 1)),
            pl.BlockSpec((WINDOW, kv_w), lambda b, j: (jnp.maximum((b * nq + j) * wpt - 1, 0), 1)),
        ],
        out_specs=pl.BlockSpec((tq, ATTN_Q_W), lambda b, j: (b * nq + j, 0)),
        out_shape=jax.ShapeDtypeStruct((n, ATTN_Q_W), BF16),
        compiler_params=pltpu.CompilerParams(dimension_semantics=("arbitrary", "arbitrary"),
                                             vmem_limit_bytes=VMEM_LIMIT),
        name="swa_sinks",
    )(sinks, aqkv, aqkv, aqkv)


def _chunk_gates(gbc_ref, gbr_ref, b, h, r0):
    c = CHUNK
    beta_c = gbc_ref[b, r0:r0 + c, h:h + 1]
    g_c = gbc_ref[b, r0:r0 + c, DN_HEADS + h:DN_HEADS + h + 1]
    beta_r = gbr_ref[b, h:h + 1, r0:r0 + c]
    g_r = gbr_ref[b, DN_HEADS + h:DN_HEADS + h + 1, r0:r0 + c]
    return beta_c, g_c, beta_r, g_r


def _delta_prep_part(q_ref, k_ref, v_ref, gbc_ref, gbr_ref, u_ref, w_ref, qg_ref, attn_ref,
                     mat_ref, at_ref, x_ref):
    batch = q_ref.shape[0]
    c = CHUNK
    n_chunks = q_ref.shape[1] // c
    items = [(cc, b, h) for cc in range(n_chunks) for b in range(batch) for h in range(DN_HEADS)]
    assert len(items) == LANES and c * 2 == LANES
    pitch = DN_MAT_PITCH

    row = lax.broadcasted_iota(jnp.int32, (c, LANES), 0)
    col = lax.broadcasted_iota(jnp.int32, (c, LANES), 1)
    causal = row >= col
    strict = row > col

    for m, (cc, b, h) in enumerate(items):
        r0, lo = cc * c, h * DN_KEY_DIM
        k = k_ref[b, r0:r0 + c, lo:lo + DN_KEY_DIM]
        q = q_ref[b, r0:r0 + c, lo:lo + DN_KEY_DIM]
        s = _dot_nt(jnp.concatenate([q, k], axis=0), jnp.concatenate([k, k], axis=0))
        beta_c, g_c, _, g_r = _chunk_gates(gbc_ref, gbr_ref, b, h, r0)
        g_cb = jnp.broadcast_to(g_c, (c, LANES))
        qg_ref[b, r0:r0 + c, lo:lo + DN_KEY_DIM] = (q.astype(F32) * jnp.exp(g_cb)).astype(BF16)
        g_r2 = jnp.concatenate([g_r, g_r], axis=1)
        decay = jnp.exp(jnp.where(causal, g_cb - g_r2, -1e30))
        attn_ref[b, h, r0:r0 + c, :] = (s[:c] * decay)[:, :c].astype(BF16)
        mat_ref[m * pitch:m * pitch + c, :] = jnp.where(strict, beta_c * s[c:] * decay, 0.0)

    for i in range(1, c):
        at_ref[i] = mat_ref[pl.ds(i, LANES, stride=pitch), :].T[:c, :]

    sub = lax.broadcasted_iota(jnp.int32, (SUBLANES, LANES), 0)
    zero_blk = jnp.zeros((SUBLANES, LANES), F32)
    for i in range(c):
        nblk = i // SUBLANES + 1
        acc = [zero_blk] * (nblk - 1) + [(sub == i % SUBLANES).astype(F32)]
        for j in range(i):
            a_ij = jnp.broadcast_to(at_ref[i, j:j + 1, :], (SUBLANES, LANES))
            for blk in range(j // SUBLANES + 1):
                acc[blk] = acc[blk] - a_ij * x_ref[j, blk * SUBLANES:(blk + 1) * SUBLANES, :]
        for blk in range(c // SUBLANES):
            x_ref[i, blk * SUBLANES:(blk + 1) * SUBLANES, :] = acc[blk] if blk < nblk else zero_blk

    zeros_half = jnp.zeros((LANES - c, LANES), F32)
    for i in range(c):
        mat_ref[pl.ds(i, LANES, stride=pitch), :] = jnp.concatenate([x_ref[i], zeros_half], axis=0).T

    for m, (cc, b, h) in enumerate(items):
        r0, lo = cc * c, h * DN_KEY_DIM
        _, _, beta_r, g_r = _chunk_gates(gbc_ref, gbr_ref, b, h, r0)
        t = mat_ref[m * pitch:m * pitch + c, :][:, :c]
        u_ref[b, r0:r0 + c, lo:lo + DN_VALUE_DIM] = _dot((t * beta_r).astype(BF16), v_ref[b, r0:r0 + c, lo:lo + DN_VALUE_DIM])
        w = _dot((t * (beta_r * jnp.exp(g_r))).astype(BF16), k_ref[b, r0:r0 + c, lo:lo + DN_KEY_DIM])
        w_ref[b, r0:r0 + c, lo:lo + DN_KEY_DIM] = w.astype(BF16)


def _delta_scan_part(qg_ref, kt_ref, u_ref, w_ref, attn_ref, z_ref, gbr_ref, nw_ref, o_ref, state_ref):
    step = pl.program_id(0)
    batch = qg_ref.shape[0]
    c = CHUNK
    ts = qg_ref.shape[1]
    per_tile = LANES // c
    assert ts % LANES == 0

    @pl.when(step == 0)
    def _():
        state_ref[...] = jnp.zeros_like(state_ref)

    nw = nw_ref[...]
    chains = [(b, h) for b in range(batch) for h in range(DN_HEADS)]
    lane = lax.broadcasted_iota(jnp.int32, (1, LANES), 1)

    k_dec_t, s_decay = {}, {}
    for (b, h) in chains:
        for t0 in range(0, ts, LANES):
            kt = kt_ref[b, h * DN_KEY_DIM:(h + 1) * DN_KEY_DIM, t0:t0 + LANES].astype(F32)
            g_r = gbr_ref[b, DN_HEADS + h:DN_HEADS + h + 1, t0:t0 + LANES]
            for sub in range(per_tile):
                cc = t0 // c + sub
                g_last = g_r[:, (sub + 1) * c - 1:(sub + 1) * c]
                in_chunk = (lane >= sub * c) & (lane < (sub + 1) * c)
                k_dec_t[(cc, b, h)] = (kt * jnp.exp(jnp.where(in_chunk, g_last - g_r, -1e30))).astype(BF16)
                s_decay[(cc, b, h)] = jnp.exp(g_last)

    for cc in range(ts // c):
        r0 = cc * c
        res, vb, om, kv = {}, {}, {}, {}
        for (b, h) in chains:
            lo = h * DN_KEY_DIM
            lhs = jnp.concatenate([w_ref[b, r0:r0 + c, lo:lo + DN_KEY_DIM],
                                   qg_ref[b, r0:r0 + c, lo:lo + DN_KEY_DIM]], axis=0)
            res[(b, h)] = _dot(lhs, state_ref[b * DN_HEADS + h].astype(BF16))
        for (b, h) in chains:
            lo = h * DN_VALUE_DIM
            vb[(b, h)] = (u_ref[b, r0:r0 + c, lo:lo + DN_VALUE_DIM] - res[(b, h)][:c]).astype(BF16)
        for (b, h) in chains:
            om[(b, h)] = _dot(attn_ref[b, h, r0:r0 + c, :], vb[(b, h)])
            kv[(b, h)] = _dot(k_dec_t[(cc, b, h)], jnp.concatenate([vb[(b, h)]] * per_tile, axis=0))
        for (b, h) in chains:
            idx = b * DN_HEADS + h
            state_ref[idx] = state_ref[idx] * s_decay[(cc, b, h)] + kv[(b, h)]
        for (b, h) in chains:
            lo = h * DN_VALUE_DIM
            o = res[(b, h)][c:] + om[(b, h)]
            o = o * lax.rsqrt(jnp.mean(o * o, axis=-1, keepdims=True) + RMS_EPS) * nw
            o = o * z_ref[b, r0:r0 + c, lo:lo + DN_VALUE_DIM].astype(F32)
            o_ref[b, r0:r0 + c, lo:lo + DN_VALUE_DIM] = o.astype(BF16)


def _delta_kernel(q_ref, k_ref, v_ref, kt_ref, z_ref, gbc_ref, gbr_ref, nw_ref, o_ref,
                  mat_ref, at_ref, x_ref, u_ref, w_ref, qg_ref, attn_ref, state_ref):
    _delta_prep_part(q_ref, k_ref, v_ref, gbc_ref, gbr_ref, u_ref, w_ref, qg_ref, attn_ref, mat_ref, at_ref, x_ref)
    _delta_scan_part(qg_ref, kt_ref, u_ref, w_ref, attn_ref, z_ref, gbr_ref, nw_ref, o_ref, state_ref)


def _delta_rule(dq, dk, dv, dkt, dzs, gbc, gbr, norm_w, batch, seq):
    ts = DN_STEP
    tok_spec = lambda width: pl.BlockSpec((batch, ts, width), lambda i: (0, i, 0))
    seq_spec = lambda rows: pl.BlockSpec((batch, rows, ts), lambda i: (0, 0, i))
    return pl.pallas_call(
        _delta_kernel,
        grid=(seq // ts,),
        in_specs=[
            tok_spec(DN_QK_W), tok_spec(DN_QK_W), tok_spec(DN_V_W), seq_spec(DN_QK_W), tok_spec(DN_V_W),
            tok_spec(LANES), seq_spec(SUBLANES), _const_spec(norm_w.shape),
        ],
        out_specs=tok_spec(DN_V_W),
        out_shape=jax.ShapeDtypeStruct((batch, seq, DN_V_W), BF16),
        scratch_shapes=[
            pltpu.VMEM((LANES * DN_MAT_PITCH, LANES), F32),
            pltpu.VMEM((CHUNK, CHUNK, LANES), F32),
            pltpu.VMEM((CHUNK, CHUNK, LANES), F32),
            pltpu.VMEM((batch, ts, DN_V_W), F32),
            pltpu.VMEM((batch, ts, DN_QK_W), BF16),
            pltpu.VMEM((batch, ts, DN_QK_W), BF16),
            pltpu.VMEM((batch, DN_HEADS, ts, CHUNK), BF16),
            pltpu.VMEM((batch * DN_HEADS, DN_KEY_DIM, DN_VALUE_DIM), F32),
        ],
        compiler_params=pltpu.CompilerParams(dimension_semantics=("arbitrary",), vmem_limit_bytes=VMEM_LIMIT),
        name="gated_delta_rule",
    )(dq, dk, dv, dkt, dzs, gbc, gbr, norm_w)


def _layer_norm(x, g, b):
    mu = jnp.mean(x, axis=-1, keepdims=True)
    xc = x - mu
    var = jnp.mean(xc * xc, axis=-1, keepdims=True)
    return xc * lax.rsqrt(var + LN_EPS) * g + b


def _merge_ffn_kernel(x_ref, oa_ref, ob_ref, wg_t_ref, wao_ref, wdo_ref, wout_f32_ref, ln1_ref,
                      wup_ref, wdown_ref, ln2_ref, out_ref, wg_ref, wout_ref):
    @pl.when(pl.program_id(0) == 0)
    def _():
        for c0 in range(0, 2 * D_MODEL, LANES):
            wg_ref[:, c0:c0 + LANES] = wg_t_ref[c0:c0 + LANES, :].T.astype(BF16)
        wout_ref[...] = wout_f32_ref[...].astype(BF16)

    x = x_ref[...]
    xb = x.astype(BF16)
    merged = _sigmoid(_dot(xb, wg_ref[:, :D_MODEL])) * _dot(oa_ref[...], wao_ref[...])
    merged = merged + _sigmoid(_dot(xb, wg_ref[:, D_MODEL:])) * _dot(ob_ref[...], wdo_ref[...])
    mix = _dot(merged.astype(BF16), wout_ref[...])
    ln1 = ln1_ref[...]
    x1 = _layer_norm(DEEPNORM_ALPHA * x + mix, ln1[0:1, :], ln1[1:2, :])
    x1b = x1.astype(BF16)
    ffn = None
    for c0 in range(0, D_FF, FFN_CHUNK):
        h = jnp.maximum(_dot(x1b, wup_ref[:, c0:c0 + FFN_CHUNK]), 0.0)
        part = _dot((h * h).astype(BF16), wdown_ref[c0:c0 + FFN_CHUNK, :])
        ffn = part if ffn is None else ffn + part
    ln2 = ln2_ref[...]
    out_ref[...] = _layer_norm(DEEPNORM_ALPHA * x1 + ffn, ln2[0:1, :], ln2[1:2, :])


def _merge_ffn(x2, oa, ob, w_t, gate_row0, wao, wdo, wout_f32, ln1, wup, wdown, ln2):
    n = x2.shape[0]
    tm = FFN_TM
    row_spec = lambda width: pl.BlockSpec((tm, width), lambda i: (i, 0))
    gate_spec = pl.BlockSpec((pl.Element(2 * D_MODEL), pl.Element(D_MODEL)), lambda i: (gate_row0, 0),
                             pipeline_mode=pl.Buffered(1))
    return pl.pallas_call(
        _merge_ffn_kernel,
        grid=(n // tm,),
        in_specs=[
            row_spec(D_MODEL), row_spec(ATTN_Q_W), row_spec(DN_V_W),
            gate_spec, _const_spec(wao.shape), _const_spec(wdo.shape), _const_spec(wout_f32.shape),
            _const_spec(ln1.shape), _const_spec(wup.shape), _const_spec(wdown.shape), _const_spec(ln2.shape),
        ],
        out_specs=row_spec(D_MODEL),
        out_shape=jax.ShapeDtypeStruct((n, D_MODEL), F32),
        scratch_shapes=[pltpu.VMEM((D_MODEL, 2 * D_MODEL), BF16), pltpu.VMEM((D_MODEL, D_MODEL), BF16)],
        compiler_params=pltpu.CompilerParams(dimension_semantics=("arbitrary",), vmem_limit_bytes=VMEM_LIMIT),
        name="merge_ffn",
    )(x2, oa, ob, w_t, wao, wdo, wout_f32, ln1, wup, wdown, ln2)


def _layer(x, w_in, conv_w, attn_sinks, dn_a_log, dn_dt_bias, dn_norm_w, w_attn_out, w_dn_out, w_out,
           ln1_g, ln1_b, w_up, w_down, ln2_g, ln2_b):
    batch, seq, _ = x.shape
    n = batch * seq
    x2 = x.reshape(n, D_MODEL)

    c_gates = ATTN_Q_W + 2 * ATTN_KV_W + DN_CONV_W + DN_V_W + 2 * DN_HEADS
    w_t = jnp.swapaxes(w_in, 0, 1).astype(F32)
    gpar = jnp.zeros((2, LANES), F32)
    gpar = gpar.at[0, DN_HEADS:2 * DN_HEADS].set(dn_a_log.astype(F32))
    gpar = gpar.at[1, DN_HEADS:2 * DN_HEADS].set(dn_dt_bias.astype(F32))

    aqkv, dq, dk, dkt, dv, dzs, gbc, gbr = _project(
        x2, w_t, conv_w.astype(F32), gpar, batch, seq)
    oa = _attention(aqkv, attn_sinks.astype(F32), batch, seq)
    r3 = lambda t: t.reshape(batch, seq, t.shape[-1])
    ob = _delta_rule(r3(dq), r3(dk), r3(dv), dkt, r3(dzs), r3(gbc), gbr,
                     dn_norm_w.astype(F32).reshape(1, DN_VALUE_DIM), batch, seq)
    out = _merge_ffn(
        x2, oa, ob.reshape(n, DN_V_W), w_t, c_gates, w_attn_out.astype(BF16), w_dn_out.astype(BF16),
        w_out.astype(F32), jnp.stack([ln1_g, ln1_b]).astype(F32), w_up.astype(BF16), w_down.astype(BF16),
        jnp.stack([ln2_g, ln2_b]).astype(F32))
    return out.reshape(batch, seq, D_MODEL)


def kernel(x, w_in, conv_w, attn_sinks, dn_a_log, dn_dt_bias, dn_norm_w, w_attn_out, w_dn_out, w_out,
           ln1_g, ln1_b, w_up, w_down, ln2_g, ln2_b):
    for l in range(DEPTH):
        x = _layer(x, w_in[l], conv_w[l], attn_sinks[l], dn_a_log[l], dn_dt_bias[l], dn_norm_w[l],
                   w_attn_out[l], w_dn_out[l], w_out[l], ln1_g[l], ln1_b[l], w_up[l], w_down[l],
                   ln2_g[l], ln2_b[l])
    return x
```

```python
import functools

import jax
import jax.numpy as jnp
from jax import lax
from jax.experimental import pallas as pl
from jax.experimental.pallas import tpu as pltpu

D_MODEL = 1024
ATTN_HEADS = 8
ATTN_KV_HEADS = 2
ATTN_HEAD_DIM = 64
WINDOW = 128
DN_HEADS = 4
DN_KEY_DIM = 128
DN_VALUE_DIM = 128
CONV_WIDTH = 4
CHUNK = 64
D_FF = 4 * D_MODEL
LN_EPS = 1e-5
RMS_EPS = 1e-6
MASKED = -1e30
DEPTH = 1
DEEPNORM_ALPHA = (2 * DEPTH) ** 0.25

ATTN_Q_W = ATTN_HEADS * ATTN_HEAD_DIM
ATTN_KV_W = ATTN_KV_HEADS * ATTN_HEAD_DIM
DN_QK_W = DN_HEADS * DN_KEY_DIM
DN_V_W = DN_HEADS * DN_VALUE_DIM
DN_CONV_W = 2 * DN_QK_W + DN_V_W

LANES = 128
SUBLANES = 8
VMEM_LIMIT = 56 * 1024 * 1024

PROJ_TM = 512
PROJ_CONV_GROUP = 2 * DN_KEY_DIM
ATTN_TQ = 2048
DN_STEP = 512
DN_MAT_PITCH = CHUNK + SUBLANES
FFN_TM = 512
FFN_CHUNK = 1024

BF16 = jnp.bfloat16
F32 = jnp.float32


def _dot(a, b):
    return jnp.dot(a, b, preferred_element_type=F32)


def _dot_nt(a, b):
    return lax.dot_general(a, b, (((1,), (1,)), ((), ())), preferred_element_type=F32)


def _sigmoid(x):
    return 0.5 + 0.5 * jnp.tanh(0.5 * x)


def _const_spec(shape):
    return pl.BlockSpec(shape, lambda *_: (0,) * len(shape), pipeline_mode=pl.Buffered(1))


def _proj_kernel(x_ref, w_t_ref, conv_ref, gpar_ref,
                 aqkv_ref, dq_ref, dk_ref, dkt_ref, dv_ref, dzs_ref, gbc_ref, gbr_ref,
                 w_attn_ref, w_dqkv_ref, w_dz_ref, w_ba_ref, *pbuf_refs, tiles_per_seq):
    i = pl.program_id(0)
    tm = x_ref.shape[0]
    xb = x_ref[...].astype(BF16)
    first = i % tiles_per_seq == 0
    half_cw = 0.5 * conv_ref[...]
    grp = PROJ_CONV_GROUP

    @pl.when(i == 0)
    def _():
        for pbuf_ref in pbuf_refs:
            pbuf_ref[tm:tm + SUBLANES, :] = jnp.zeros((SUBLANES, grp), F32)
        def w_cols(c0, width=LANES):
            return w_t_ref[c0:c0 + width, :].T

        lo_half = lax.broadcasted_iota(jnp.int32, (D_MODEL, LANES), 1) < ATTN_HEAD_DIM
        for c0 in range(0, ATTN_Q_W, LANES):
            w_attn_ref[:, c0:c0 + LANES] = (w_cols(c0) * (ATTN_HEAD_DIM ** -0.5)).astype(BF16)
        for n, c0 in enumerate((ATTN_Q_W, ATTN_Q_W + ATTN_KV_W)):
            pair = w_cols(c0, ATTN_KV_W)
            swapped = pltpu.roll(pair, ATTN_HEAD_DIM, 1)
            base = ATTN_Q_W + 2 * n * ATTN_KV_W
            w_attn_ref[:, base:base + LANES] = jnp.where(lo_half, pair, swapped).astype(BF16)
            w_attn_ref[:, base + LANES:base + 2 * LANES] = jnp.where(lo_half, swapped, pair).astype(BF16)
        c_dqkv = ATTN_Q_W + 2 * ATTN_KV_W
        for c0 in range(0, DN_CONV_W, LANES):
            w_dqkv_ref[:, c0:c0 + LANES] = w_cols(c_dqkv + c0).astype(BF16)
        c_dz = c_dqkv + DN_CONV_W
        for c0 in range(0, DN_V_W, LANES):
            w_dz_ref[:, c0:c0 + LANES] = w_cols(c_dz + c0).astype(BF16)
        c_ba = c_dz + DN_V_W
        keep = lax.broadcasted_iota(jnp.int32, (D_MODEL, LANES), 1) < 2 * DN_HEADS
        w_ba_ref[...] = jnp.where(keep, w_cols(c_ba), 0.0).astype(BF16)

    def silu_of_twice(h):
        return h + h * jnp.tanh(h)

    def conv_matmul(gi):
        pbuf_ref = pbuf_refs[gi]
        carry = pbuf_ref[tm:tm + SUBLANES, :]
        pbuf_ref[0:SUBLANES, :] = jnp.where(first, 0.0, carry)
        pbuf_ref[SUBLANES:SUBLANES + tm, :] = _dot(xb, w_dqkv_ref[:, gi * grp:(gi + 1) * grp])

    def conv_silu(gi):
        h = None
        for s in range(CONV_WIDTH):
            taps = half_cw[CONV_WIDTH - 1 - s:CONV_WIDTH - s, gi * grp:(gi + 1) * grp]
            term = pbuf_refs[gi][pl.ds(SUBLANES - s, tm), :] * taps
            h = term if h is None else h + term
        return silu_of_twice(h)

    def unit_rows(yh, scale):
        return yh * (lax.rsqrt(jnp.sum(yh * yh, axis=-1, keepdims=True) + RMS_EPS) * scale)

    def q_tail(c0, _):
        y = conv_silu(c0 // grp)
        for lo in range(0, grp, DN_KEY_DIM):
            dq_ref[:, c0 + lo:c0 + lo + DN_KEY_DIM] = unit_rows(y[:, lo:lo + DN_KEY_DIM], DN_KEY_DIM ** -0.5).astype(BF16)

    def k_tail(c0, _):
        y = conv_silu((DN_QK_W + c0) // grp)
        for lo in range(0, grp, DN_KEY_DIM):
            kh = unit_rows(y[:, lo:lo + DN_KEY_DIM], 1.0).astype(BF16)
            dk_ref[:, c0 + lo:c0 + lo + DN_KEY_DIM] = kh
            dkt_ref[c0 + lo:c0 + lo + DN_KEY_DIM, :] = kh.astype(F32).T.astype(BF16)

    def v_tail(c0, _):
        dv_ref[:, c0:c0 + grp] = conv_silu((2 * DN_QK_W + c0) // grp).astype(BF16)

    def attn_tail(c0, a):
        aqkv_ref[:, c0:c0 + ATTN_Q_W] = a.astype(BF16)

    def dz_tail(_, a):
        dzs_ref[...] = silu_of_twice(0.5 * a).astype(BF16)

    def gate_tail(_, ba):
        gpar = gpar_ref[...]
        lane = lax.broadcasted_iota(jnp.int32, (tm, LANES), 1)
        row = lax.broadcasted_iota(jnp.int32, (tm, LANES), 0)
        beta = _sigmoid(ba)
        z = ba + gpar[1:2, :]
        softplus = jnp.maximum(z, 0.0) + jnp.log1p(jnp.exp(-jnp.abs(z)))
        g = -jnp.exp(gpar[0:1, :]) * softplus
        s = 1
        while s < CHUNK:
            g = g + jnp.where(row % CHUNK >= s, pltpu.roll(g, s, 0), 0.0)
            s *= 2
        gb = jnp.where(lane < DN_HEADS, beta, jnp.where(lane < 2 * DN_HEADS, g, 0.0))
        gbc_ref[...] = gb
        gbr_ref[...] = gb.T[:SUBLANES, :]

    conv_stages = []
    for c0 in range(0, DN_QK_W, grp):
        conv_stages.append((functools.partial(conv_matmul, c0 // grp), q_tail, c0))
    for c0 in range(0, DN_QK_W, grp):
        conv_stages.append((functools.partial(conv_matmul, (DN_QK_W + c0) // grp), k_tail, c0))
    for c0 in range(0, DN_V_W, grp):
        conv_stages.append((functools.partial(conv_matmul, (2 * DN_QK_W + c0) // grp), v_tail, c0))
    plain_stages = []
    for c0 in range(0, w_attn_ref.shape[1], ATTN_Q_W):
        plain_stages.append((functools.partial(lambda c: _dot(xb, w_attn_ref[:, c:c + ATTN_Q_W]), c0), attn_tail, c0))
    plain_stages.append((lambda: _dot(xb, w_dz_ref[...]), dz_tail, 0))
    plain_stages.append((lambda: _dot(xb, w_ba_ref[...]), gate_tail, 0))
    stages = []
    for n, st in enumerate(conv_stages):
        stages.append(st)
        if n < len(plain_stages):
            stages.append(plain_stages[n])
    stages += plain_stages[len(conv_stages):]

    pending = stages[0][0]()
    for n, (_, tail, c0) in enumerate(stages):
        ready = pending
        if n + 1 < len(stages):
            pending = stages[n + 1][0]()
        tail(c0, ready)


def _project(x2, w_t, conv_w, gpar, batch, seq):
    n = x2.shape[0]
    tm = PROJ_TM
    tiles_per_seq = seq // tm
    attn_w = ATTN_Q_W + 4 * ATTN_KV_W
    row_spec = lambda width: pl.BlockSpec((tm, width), lambda i: (i, 0))
    seq_spec = lambda rows: pl.BlockSpec((None, rows, tm), lambda i: (i // tiles_per_seq, 0, i % tiles_per_seq))
    out_shape = (
        jax.ShapeDtypeStruct((n, attn_w), BF16),
        jax.ShapeDtypeStruct((n, DN_QK_W), BF16),
        jax.ShapeDtypeStruct((n, DN_QK_W), BF16),
        jax.ShapeDtypeStruct((batch, DN_QK_W, seq), BF16),
        jax.ShapeDtypeStruct((n, DN_V_W), BF16),
        jax.ShapeDtypeStruct((n, DN_V_W), BF16),
        jax.ShapeDtypeStruct((n, LANES), F32),
        jax.ShapeDtypeStruct((batch, SUBLANES, seq), F32),
    )
    return pl.pallas_call(
        functools.partial(_proj_kernel, tiles_per_seq=tiles_per_seq),
        grid=(n // tm,),
        in_specs=[
            row_spec(D_MODEL),
            _const_spec(w_t.shape), _const_spec(conv_w.shape), _const_spec(gpar.shape),
        ],
        out_specs=(
            row_spec(attn_w), row_spec(DN_QK_W), row_spec(DN_QK_W), seq_spec(DN_QK_W),
            row_spec(DN_V_W), row_spec(DN_V_W), row_spec(LANES), seq_spec(SUBLANES),
        ),
        out_shape=out_shape,
        scratch_shapes=[pltpu.VMEM((D_MODEL, attn_w), BF16), pltpu.VMEM((D_MODEL, DN_CONV_W), BF16),
                        pltpu.VMEM((D_MODEL, DN_V_W), BF16), pltpu.VMEM((D_MODEL, LANES), BF16)]
        + [pltpu.VMEM((SUBLANES + tm, PROJ_CONV_GROUP), F32)] * (DN_CONV_W // PROJ_CONV_GROUP),
        compiler_params=pltpu.CompilerParams(dimension_semantics=("arbitrary",), vmem_limit_bytes=VMEM_LIMIT),
        name="proj_dnprep",
    )(x2, w_t, conv_w, gpar)


def _attn_kernel(sink_ref, q_ref, kv_ref, kvp_ref, o_ref):
    j = pl.program_id(1)
    tq = q_ref.shape[0]
    w = WINDOW
    pair_w = 2 * ATTN_HEAD_DIM
    group = ATTN_HEADS // ATTN_KV_HEADS
    lo = lax.broadcasted_iota(jnp.int32, (w, pair_w), 1) < ATTN_HEAD_DIM
    qi = lax.broadcasted_iota(jnp.int32, (w, 2 * w), 0)
    kj = lax.broadcasted_iota(jnp.int32, (w, 2 * w), 1)
    band = (kj > qi) & (kj <= qi + w)
    zero = jnp.zeros((), BF16)

    def two_windows(wi, col):
        cols = slice(col * pair_w, (col + 1) * pair_w)
        prev = kvp_ref[:, cols] if wi == 0 else kv_ref[(wi - 1) * w:wi * w, cols]
        return jnp.concatenate([prev, kv_ref[wi * w:(wi + 1) * w, cols]], axis=0)

    def scores(wi, hk):
        parts = []
        for pp in range(group // 2):
            pair = hk * (group // 2) + pp
            qp = q_ref[wi * w:(wi + 1) * w, pair * pair_w:(pair + 1) * pair_w]
            parts += [jnp.where(lo, qp, zero), jnp.where(lo, zero, qp)]
        return _dot_nt(jnp.concatenate(parts, axis=0), two_windows(wi, hk))

    def finish(wi, hk, s):
        mask = band & ((kj >= w) | (j > 0)) if wi == 0 else band
        ps, invs = [], []
        for e in range(group):
            se = jnp.where(mask, s[e * w:(e + 1) * w], MASKED)
            sink = sink_ref[hk * group + e]
            m = jnp.maximum(jnp.max(se, axis=-1, keepdims=True), sink)
            pe = jnp.exp(se - m)
            den = jnp.sum(pe, axis=-1, keepdims=True) + jnp.exp(sink - m)
            ps.append(pe.astype(BF16))
            invs.append(1.0 / den)
        o = _dot(jnp.concatenate(ps, axis=0), two_windows(wi, ATTN_KV_HEADS + hk))
        for pp in range(group // 2):
            pair = hk * (group // 2) + pp
            oa = o[(2 * pp) * w:(2 * pp + 1) * w] * invs[2 * pp]
            ob = o[(2 * pp + 1) * w:(2 * pp + 2) * w] * invs[2 * pp + 1]
            o_ref[wi * w:(wi + 1) * w, pair * pair_w:(pair + 1) * pair_w] = jnp.where(lo, oa, ob).astype(BF16)

    units = [(wi, hk) for wi in range(tq // w) for hk in range(ATTN_KV_HEADS)]
    s_next = scores(*units[0])
    for n, unit in enumerate(units):
        s = s_next
        if n + 1 < len(units):
            s_next = scores(*units[n + 1])
        finish(*unit, s)


def _attention(aqkv, sinks, batch, seq):
    n = aqkv.shape[0]
    tq = ATTN_TQ
    nq = seq // tq
    wpt = tq // WINDOW
    kv_w = aqkv.shape[1] - ATTN_Q_W
    assert kv_w == ATTN_Q_W
    return pl.pallas_call(
        _attn_kernel,
        grid=(batch, nq),
        in_specs=[
            pl.BlockSpec(memory_space=pltpu.SMEM),
            pl.BlockSpec((tq, ATTN_Q_W), lambda b, j: (b * nq + j, 0)),
            pl.BlockSpec((tq, kv_w), lambda b, j: (b * nq + j, 1)),
            pl.BlockSpec((WINDOW, kv_w), lambda b, j: (jnp.maximum((b * nq + j) * wpt - 1, 0), 1)),
        ],
        out_specs=pl.BlockSpec((tq, ATTN_Q_W), lambda b, j: (b * nq + j, 0)),
        out_shape=jax.ShapeDtypeStruct((n, ATTN_Q_W), BF16),
        compiler_params=pltpu.CompilerParams(dimension_semantics=("arbitrary", "arbitrary"),
                                             vmem_limit_bytes=VMEM_LIMIT),
        name="swa_sinks",
    )(sinks, aqkv, aqkv, aqkv)


def _chunk_gates(gbc_ref, gbr_ref, b, h, r0):
    c = CHUNK
    beta_c = gbc_ref[b, r0:r0 + c, h:h + 1]
    g_c = gbc_ref[b, r0:r0 + c, DN_HEADS + h:DN_HEADS + h + 1]
    beta_r = gbr_ref[b, h:h + 1, r0:r0 + c]
    g_r = gbr_ref[b, DN_HEADS + h:DN_HEADS + h + 1, r0:r0 + c]
    return beta_c, g_c, beta_r, g_r


def _delta_prep_part(q_ref, k_ref, v_ref, gbc_ref, gbr_ref, u_ref, w_ref, qg_ref, attn_ref,
                     mat_ref, at_ref, x_ref):
    batch = q_ref.shape[0]
    c = CHUNK
    n_chunks = q_ref.shape[1] // c
    items = [(cc, b, h) for cc in range(n_chunks) for b in range(batch) for h in range(DN_HEADS)]
    assert len(items) == LANES and c * 2 == LANES
    pitch = DN_MAT_PITCH

    row = lax.broadcasted_iota(jnp.int32, (c, LANES), 0)
    col = lax.broadcasted_iota(jnp.int32, (c, LANES), 1)
    causal = row >= col
    strict = row > col

    for m, (cc, b, h) in enumerate(items):
        r0, lo = cc * c, h * DN_KEY_DIM
        k = k_ref[b, r0:r0 + c, lo:lo + DN_KEY_DIM]
        q = q_ref[b, r0:r0 + c, lo:lo + DN_KEY_DIM]
        s = _dot_nt(jnp.concatenate([q, k], axis=0), jnp.concatenate([k, k], axis=0))
        beta_c, g_c, _, g_r = _chunk_gates(gbc_ref, gbr_ref, b, h, r0)
        g_cb = jnp.broadcast_to(g_c, (c, LANES))
        qg_ref[b, r0:r0 + c, lo:lo + DN_KEY_DIM] = (q.astype(F32) * jnp.exp(g_cb)).astype(BF16)
        g_r2 = jnp.concatenate([g_r, g_r], axis=1)
        decay = jnp.exp(jnp.where(causal, g_cb - g_r2, MASKED))
        attn_ref[b, h, r0:r0 + c, :] = (s[:c] * decay)[:, :c].astype(BF16)
        mat_ref[m * pitch:m * pitch + c, :] = jnp.where(strict, beta_c * s[c:] * decay, 0.0)

    for i in range(1, c):
        at_ref[i] = mat_ref[pl.ds(i, LANES, stride=pitch), :].T[:c, :]

    sub = lax.broadcasted_iota(jnp.int32, (SUBLANES, LANES), 0)
    zero_blk = jnp.zeros((SUBLANES, LANES), F32)
    for i in range(c):
        nblk = i // SUBLANES + 1
        acc = [zero_blk] * (nblk - 1) + [(sub == i % SUBLANES).astype(F32)]
        for j in range(i):
            a_ij = jnp.broadcast_to(at_ref[i, j:j + 1, :], (SUBLANES, LANES))
            for blk in range(j // SUBLANES + 1):
                acc[blk] = acc[blk] - a_ij * x_ref[j, blk * SUBLANES:(blk + 1) * SUBLANES, :]
        for blk in range(c // SUBLANES):
            x_ref[i, blk * SUBLANES:(blk + 1) * SUBLANES, :] = acc[blk] if blk < nblk else zero_blk

    zeros_half = jnp.zeros((LANES - c, LANES), F32)
    for i in range(c):
        mat_ref[pl.ds(i, LANES, stride=pitch), :] = jnp.concatenate([x_ref[i], zeros_half], axis=0).T

    for m, (cc, b, h) in enumerate(items):
        r0, lo = cc * c, h * DN_KEY_DIM
        _, _, beta_r, g_r = _chunk_gates(gbc_ref, gbr_ref, b, h, r0)
        t = mat_ref[m * pitch:m * pitch + c, :][:, :c]
        u_ref[b, r0:r0 + c, lo:lo + DN_VALUE_DIM] = _dot((t * beta_r).astype(BF16), v_ref[b, r0:r0 + c, lo:lo + DN_VALUE_DIM])
        w = _dot((t * (beta_r * jnp.exp(g_r))).astype(BF16), k_ref[b, r0:r0 + c, lo:lo + DN_KEY_DIM])
        w_ref[b, r0:r0 + c, lo:lo + DN_KEY_DIM] = w.astype(BF16)


def _delta_scan_part(qg_ref, kt_ref, u_ref, w_ref, attn_ref, z_ref, gbr_ref, nw_ref, o_ref, state_ref):
    step = pl.program_id(0)
    batch = qg_ref.shape[0]
    c = CHUNK
    ts = qg_ref.shape[1]
    per_tile = LANES // c
    assert ts % LANES == 0

    @pl.when(step == 0)
    def _():
        state_ref[...] = jnp.zeros_like(state_ref)

    nw = nw_ref[...]
    chains = [(b, h) for b in range(batch) for h in range(DN_HEADS)]
    lane = lax.broadcasted_iota(jnp.int32, (1, LANES), 1)

    k_dec_t, s_decay = {}, {}
    for (b, h) in chains:
        for t0 in range(0, ts, LANES):
            kt = kt_ref[b, h * DN_KEY_DIM:(h + 1) * DN_KEY_DIM, t0:t0 + LANES].astype(F32)
            g_r = gbr_ref[b, DN_HEADS + h:DN_HEADS + h + 1, t0:t0 + LANES]
            for sub in range(per_tile):
                cc = t0 // c + sub
                g_last = g_r[:, (sub + 1) * c - 1:(sub + 1) * c]
                in_chunk = (lane >= sub * c) & (lane < (sub + 1) * c)
                k_dec_t[(cc, b, h)] = (kt * jnp.exp(jnp.where(in_chunk, g_last - g_r, MASKED))).astype(BF16)
                s_decay[(cc, b, h)] = jnp.exp(g_last)

    for cc in range(ts // c):
        r0 = cc * c
        res, vb, om, kv = {}, {}, {}, {}
        for (b, h) in chains:
            lo = h * DN_KEY_DIM
            lhs = jnp.concatenate([w_ref[b, r0:r0 + c, lo:lo + DN_KEY_DIM],
                                   qg_ref[b, r0:r0 + c, lo:lo + DN_KEY_DIM]], axis=0)
            res[(b, h)] = _dot(lhs, state_ref[b * DN_HEADS + h].astype(BF16))
        for (b, h) in chains:
            lo = h * DN_VALUE_DIM
            vb[(b, h)] = (u_ref[b, r0:r0 + c, lo:lo + DN_VALUE_DIM] - res[(b, h)][:c]).astype(BF16)
        for (b, h) in chains:
            om[(b, h)] = _dot(attn_ref[b, h, r0:r0 + c, :], vb[(b, h)])
            kv[(b, h)] = _dot(k_dec_t[(cc, b, h)], jnp.concatenate([vb[(b, h)]] * per_tile, axis=0))
        for (b, h) in chains:
            idx = b * DN_HEADS + h
            state_ref[idx] = state_ref[idx] * s_decay[(cc, b, h)] + kv[(b, h)]
        for (b, h) in chains:
            lo = h * DN_VALUE_DIM
            o = res[(b, h)][c:] + om[(b, h)]
            o = o * lax.rsqrt(jnp.mean(o * o, axis=-1, keepdims=True) + RMS_EPS) * nw
            o = o * z_ref[b, r0:r0 + c, lo:lo + DN_VALUE_DIM].astype(F32)
            o_ref[b, r0:r0 + c, lo:lo + DN_VALUE_DIM] = o.astype(BF16)


def _delta_kernel(q_ref, k_ref, v_ref, kt_ref, z_ref, gbc_ref, gbr_ref, nw_ref, o_ref,
                  mat_ref, at_ref, x_ref, u_ref, w_ref, qg_ref, attn_ref, state_ref):
    _delta_prep_part(q_ref, k_ref, v_ref, gbc_ref, gbr_ref, u_ref, w_ref, qg_ref, attn_ref, mat_ref, at_ref, x_ref)
    _delta_scan_part(qg_ref, kt_ref, u_ref, w_ref, attn_ref, z_ref, gbr_ref, nw_ref, o_ref, state_ref)


def _delta_rule(dq, dk, dv, dkt, dzs, gbc, gbr, norm_w, batch, seq):
    ts = DN_STEP
    tok_spec = lambda width: pl.BlockSpec((batch, ts, width), lambda i: (0, i, 0))
    seq_spec = lambda rows: pl.BlockSpec((batch, rows, ts), lambda i: (0, 0, i))
    return pl.pallas_call(
        _delta_kernel,
        grid=(seq // ts,),
        in_specs=[
            tok_spec(DN_QK_W), tok_spec(DN_QK_W), tok_spec(DN_V_W), seq_spec(DN_QK_W), tok_spec(DN_V_W),
            tok_spec(LANES), seq_spec(SUBLANES), _const_spec(norm_w.shape),
        ],
        out_specs=tok_spec(DN_V_W),
        out_shape=jax.ShapeDtypeStruct((batch, seq, DN_V_W), BF16),
        scratch_shapes=[
            pltpu.VMEM((LANES * DN_MAT_PITCH, LANES), F32),
            pltpu.VMEM((CHUNK, CHUNK, LANES), F32),
            pltpu.VMEM((CHUNK, CHUNK, LANES), F32),
            pltpu.VMEM((batch, ts, DN_V_W), F32),
            pltpu.VMEM((batch, ts, DN_QK_W), BF16),
            pltpu.VMEM((batch, ts, DN_QK_W), BF16),
            pltpu.VMEM((batch, DN_HEADS, ts, CHUNK), BF16),
            pltpu.VMEM((batch * DN_HEADS, DN_KEY_DIM, DN_VALUE_DIM), F32),
        ],
        compiler_params=pltpu.CompilerParams(dimension_semantics=("arbitrary",), vmem_limit_bytes=VMEM_LIMIT),
        name="gated_delta_rule",
    )(dq, dk, dv, dkt, dzs, gbc, gbr, norm_w)


def _layer_norm(x, g, b):
    mu = jnp.mean(x, axis=-1, keepdims=True)
    xc = x - mu
    var = jnp.mean(xc * xc, axis=-1, keepdims=True)
    return xc * lax.rsqrt(var + LN_EPS) * g + b


def _merge_ffn_kernel(x_ref, oa_ref, ob_ref, wg_t_ref, wao_ref, wdo_ref, wout_f32_ref, ln1_ref,
                      wup_ref, wdown_ref, ln2_ref, out_ref, wg_ref, wout_ref):
    @pl.when(pl.program_id(0) == 0)
    def _():
        for c0 in range(0, 2 * D_MODEL, LANES):
            wg_ref[:, c0:c0 + LANES] = wg_t_ref[c0:c0 + LANES, :].T.astype(BF16)
        wout_ref[...] = wout_f32_ref[...].astype(BF16)

    x = x_ref[...]
    xb = x.astype(BF16)
    merged = _sigmoid(_dot(xb, wg_ref[:, :D_MODEL])) * _dot(oa_ref[...], wao_ref[...])
    merged = merged + _sigmoid(_dot(xb, wg_ref[:, D_MODEL:])) * _dot(ob_ref[...], wdo_ref[...])
    mix = _dot(merged.astype(BF16), wout_ref[...])
    ln1 = ln1_ref[...]
    x1 = _layer_norm(DEEPNORM_ALPHA * x + mix, ln1[0:1, :], ln1[1:2, :])
    x1b = x1.astype(BF16)
    ffn = None
    for c0 in range(0, D_FF, FFN_CHUNK):
        h = jnp.maximum(_dot(x1b, wup_ref[:, c0:c0 + FFN_CHUNK]), 0.0)
        part = _dot((h * h).astype(BF16), wdown_ref[c0:c0 + FFN_CHUNK, :])
        ffn = part if ffn is None else ffn + part
    ln2 = ln2_ref[...]
    out_ref[...] = _layer_norm(DEEPNORM_ALPHA * x1 + ffn, ln2[0:1, :], ln2[1:2, :])


def _merge_ffn(x2, oa, ob, w_t, gate_row0, wao, wdo, wout_f32, ln1, wup, wdown, ln2):
    n = x2.shape[0]
    tm = FFN_TM
    row_spec = lambda width: pl.BlockSpec((tm, width), lambda i: (i, 0))
    gate_spec = pl.BlockSpec((pl.Element(2 * D_MODEL), pl.Element(D_MODEL)), lambda i: (gate_row0, 0),
                             pipeline_mode=pl.Buffered(1))
    return pl.pallas_call(
        _merge_ffn_kernel,
        grid=(n // tm,),
        in_specs=[
            row_spec(D_MODEL), row_spec(ATTN_Q_W), row_spec(DN_V_W),
            gate_spec, _const_spec(wao.shape), _const_spec(wdo.shape), _const_spec(wout_f32.shape),
            _const_spec(ln1.shape), _const_spec(wup.shape), _const_spec(wdown.shape), _const_spec(ln2.shape),
        ],
        out_specs=row_spec(D_MODEL),
        out_shape=jax.ShapeDtypeStruct((n, D_MODEL), F32),
        scratch_shapes=[pltpu.VMEM((D_MODEL, 2 * D_MODEL), BF16), pltpu.VMEM((D_MODEL, D_MODEL), BF16)],
        compiler_params=pltpu.CompilerParams(dimension_semantics=("arbitrary",), vmem_limit_bytes=VMEM_LIMIT),
        name="merge_ffn",
    )(x2, oa, ob, w_t, wao, wdo, wout_f32, ln1, wup, wdown, ln2)


def _layer(x, w_in, conv_w, attn_sinks, dn_a_log, dn_dt_bias, dn_norm_w, w_attn_out, w_dn_out, w_out,
           ln1_g, ln1_b, w_up, w_down, ln2_g, ln2_b):
    batch, seq, _ = x.shape
    n = batch * seq
    x2 = x.reshape(n, D_MODEL)

    c_gates = ATTN_Q_W + 2 * ATTN_KV_W + DN_CONV_W + DN_V_W + 2 * DN_HEADS
    w_t = jnp.swapaxes(w_in, 0, 1).astype(F32)
    gpar = jnp.zeros((2, LANES), F32)
    gpar = gpar.at[0, DN_HEADS:2 * DN_HEADS].set(dn_a_log.astype(F32))
    gpar = gpar.at[1, DN_HEADS:2 * DN_HEADS].set(dn_dt_bias.astype(F32))

    aqkv, dq, dk, dkt, dv, dzs, gbc, gbr = _project(
        x2, w_t, conv_w.astype(F32), gpar, batch, seq)
    oa = _attention(aqkv, attn_sinks.astype(F32), batch, seq)
    r3 = lambda t: t.reshape(batch, seq, t.shape[-1])
    ob = _delta_rule(r3(dq), r3(dk), r3(dv), dkt, r3(dzs), r3(gbc), gbr,
                     dn_norm_w.astype(F32).reshape(1, DN_VALUE_DIM), batch, seq)
    out = _merge_ffn(
        x2, oa, ob.reshape(n, DN_V_W), w_t, c_gates, w_attn_out.astype(BF16), w_dn_out.astype(BF16),
        w_out.astype(F32), jnp.stack([ln1_g, ln1_b]).astype(F32), w_up.astype(BF16), w_down.astype(BF16),
        jnp.stack([ln2_g, ln2_b]).astype(F32))
    return out.reshape(batch, seq, D_MODEL)


def kernel(x, w_in, conv_w, attn_sinks, dn_a_log, dn_dt_bias, dn_norm_w, w_attn_out, w_dn_out, w_out,
           ln1_g, ln1_b, w_up, w_down, ln2_g, ln2_b):
    for l in range(DEPTH):
        x = _layer(x, w_in[l], conv_w[l], attn_sinks[l], dn_a_log[l], dn_dt_bias[l], dn_norm_w[l],
                   w_attn_out[l], w_dn_out[l], w_out[l], ln1_g[l], ln1_b[l], w_up[l], w_down[l],
                   ln2_g[l], ln2_b[l])
    return x
```
